```python
import jax, jax.numpy as jnp
from jax import lax
import numpy as np

D_MODEL = 2048
BATCH = 8
SEQ = 4096
DEPTH = 2

GRID_W = 64
CTX_LEN = 256
N_BRANCH = 4
WA = 1024
CONV_A = 3
MLA_HEADS = 8
Q_LORA = 768
KV_LORA = 512
NOPE_DIM = 128
ROPE_DIM = 64
V_DIM = 128
QK_DIM = NOPE_DIM + ROPE_DIM
ROPE_THETA = 10000.0
Q_BLOCK = 128
WC = 1024
POOL_WINDOWS = (2, 4, 8, 16)
POOL_GROUP = WC // len(POOL_WINDOWS)
POOL_OUT = D_MODEL // len(POOL_WINDOWS)
WD = 1024
CONV_D = 31
D_FF = 5632
CONV_FF = 3
EPS = 1e-6
LN_EPS = 1e-5

N_A = 3 * WA
N_B = Q_LORA + KV_LORA + ROPE_DIM
N_C = WC
N_D = 2 * WD
N_G = N_BRANCH * D_MODEL
N_IN = N_A + N_B + N_C + N_D + N_G
OFF_Q = N_A
OFF_KV = OFF_Q + Q_LORA
OFF_KR = OFF_KV + KV_LORA
OFF_P = OFF_KR + ROPE_DIM
OFF_D = OFF_P + N_C
OFF_G = OFF_D + N_D
IN_SPLITS = (OFF_Q, OFF_KV, OFF_KR, OFF_P, OFF_D, OFF_G)

kernel_name = 'hybrid_parallel_dit_block'


def rms_norm(x, g):
    xf = x.astype(jnp.float32)
    y = xf * lax.rsqrt(jnp.mean(xf * xf, axis=-1, keepdims=True) + EPS)
    return (y * g.astype(jnp.float32)).astype(x.dtype)


def layer_norm(x, g, b):
    xf = x.astype(jnp.float32)
    mu = jnp.mean(xf, axis=-1, keepdims=True)
    var = jnp.mean(jnp.square(xf - mu), axis=-1, keepdims=True)
    y = (xf - mu) * lax.rsqrt(var + LN_EPS)
    return (y * g.astype(jnp.float32) + b.astype(jnp.float32)).astype(x.dtype)


def modulate(h, shift, scale):
    return h * (1 + scale) + shift


def dwconv(x, w):
    k = w.shape[0]
    return lax.conv_general_dilated(
        x, w[:, None, :].astype(x.dtype), window_strides=(1,),
        padding=[(k // 2, k // 2)], dimension_numbers=('NWC', 'WIO', 'NWC'),
        feature_group_count=x.shape[-1])


def axial_rope_tables(L, dtype):
    rows = L // GRID_W
    row = jnp.broadcast_to(jnp.arange(rows)[:, None], (rows, GRID_W)).reshape(L)
    col = jnp.broadcast_to(jnp.arange(GRID_W)[None, :], (rows, GRID_W)).reshape(L)
    n_freq = ROPE_DIM // 4
    inv = ROPE_THETA ** (-jnp.arange(n_freq, dtype=jnp.float32) / n_freq)
    ang = jnp.stack([row, col], axis=-1).astype(jnp.float32)[:, :, None] * inv
    return jnp.cos(ang).astype(dtype), jnp.sin(ang).astype(dtype)


def apply_rope(x, rope):
    cos, sin = rope
    B, L, H, _ = x.shape
    nf = ROPE_DIM // 4
    r = x[..., NOPE_DIM:].reshape(B, L, H, 2, 2, nf)
    r1, r2 = r[..., 0, :], r[..., 1, :]
    cs, sn = cos[None, :, None], sin[None, :, None]
    rot = jnp.stack([r1 * cs - r2 * sn, r1 * sn + r2 * cs], axis=-2).reshape(B, L, H, ROPE_DIM)
    return jnp.concatenate([x[..., :NOPE_DIM], rot], axis=-1)


def short_conv(u, conv_w, w_out):
    b, cg, hh = jnp.split(u, 3, axis=-1)
    return (b * dwconv(cg * hh, conv_w)) @ w_out


def mla_q(zq, q_norm_g, w_q_up, q_head_g, rope):
    B, L, _ = zq.shape
    q = (rms_norm(zq, q_norm_g) @ w_q_up).reshape(B, L, MLA_HEADS, QK_DIM)
    q = rms_norm(q, q_head_g)
    if rope is not None:
        q = apply_rope(q, rope)
    return q


def mla_kv(zkv, zkr, kv_norm_g, w_kv_up, k_head_g, rope):
    B, L, _ = zkv.shape
    kv = (rms_norm(zkv, kv_norm_g) @ w_kv_up).reshape(B, L, MLA_HEADS, NOPE_DIM + V_DIM)
    k_nope, v = kv[..., :NOPE_DIM], kv[..., NOPE_DIM:]
    k_rope = jnp.broadcast_to(zkr[:, :, None, :], (B, L, MLA_HEADS, ROPE_DIM))
    k = rms_norm(jnp.concatenate([k_nope, k_rope], axis=-1), k_head_g)
    if rope is not None:
        k = apply_rope(k, rope)
    return k, v


def softmax_attention(q, k, v):
    s = jnp.einsum('bqhd,bkhd->bhqk', q, k).astype(jnp.float32) * (QK_DIM ** -0.5)
    p = jax.nn.softmax(s, axis=-1).astype(v.dtype)
    return jnp.einsum('bhqk,bkhd->bqhd', p, v)


def latent_attention(q, k_all, v_all):
    B, L, H, Dq = q.shape
    nb = L // Q_BLOCK
    qb = jnp.moveaxis(q.reshape(B, nb, Q_BLOCK, H, Dq), 1, 0)
    ob = lax.map(lambda qi: softmax_attention(qi, k_all, v_all), qb)
    return jnp.moveaxis(ob, 0, 1).reshape(B, L, H * V_DIM)


def pool_mixer(u, w_pool, pool_scale):
    L = u.shape[1]
    uf = u.astype(jnp.float32)
    cs = jnp.pad(jnp.cumsum(uf, axis=1), ((0, 0), (1, 0), (0, 0)))
    t = jnp.arange(L)
    outs = []
    for g, w in enumerate(POOL_WINDOWS):
        lo = jnp.clip(t - w // 2, 0, L)
        hi = jnp.clip(t - w // 2 + w, 0, L)
        csg = cs[:, :, g * POOL_GROUP:(g + 1) * POOL_GROUP]
        win_sum = jnp.take(csg, hi, axis=1) - jnp.take(csg, lo, axis=1)
        mean = win_sum / (hi - lo).astype(jnp.float32)[None, :, None]
        d = (mean - uf[:, :, g * POOL_GROUP:(g + 1) * POOL_GROUP]).astype(u.dtype)
        outs.append(d @ w_pool[g])
    return jnp.concatenate(outs, axis=-1) * pool_scale


def conformer_conv(u, conv_w, conv_b, ln_g, ln_b, w_out):
    a, gt = jnp.split(u, 2, axis=-1)
    y = a * jax.nn.sigmoid(gt)
    y = dwconv(y, conv_w) + conv_b
    y = jax.nn.silu(layer_norm(y, ln_g, ln_b))
    return y @ w_out


def gated_merge(zg, ys, w_out):
    gates = jax.nn.sigmoid(zg)
    merged = gates[..., :D_MODEL] * ys[0]
    for i in range(1, N_BRANCH):
        merged = merged + gates[..., i * D_MODEL:(i + 1) * D_MODEL] * ys[i]
    return merged @ w_out


def conv_ffn(h, w_up, conv_w, conv_b, w_down):
    gate, val = jnp.split(h @ w_up, 2, axis=-1)
    gate = dwconv(gate, conv_w) + conv_b
    return (jax.nn.silu(gate) * val) @ w_down


def setup_inputs(seed: int = 0) -> dict:
    key = jax.random.key(seed)
    ks = iter(jax.random.split(key, 40))
    D = D_MODEL

    def nrm(shape, scale):
        return jax.random.normal(next(ks), shape, jnp.float32) * scale

    def gain(shape):
        return 1.0 + 0.02 * jax.random.normal(next(ks), shape, jnp.float32)

    return {
        'x': nrm((BATCH, SEQ, D), 1.0),
        'c': nrm((BATCH, D), 1.0),
        'ctx': nrm((BATCH, CTX_LEN, D), 1.0),
        'c_ctx': nrm((D,), 1.0),
        'ada_w': nrm((DEPTH, D, 6 * D), 0.5 * D ** -0.5),
        'ada_b': nrm((DEPTH, 6 * D), 0.02),
        'norm1_g': gain((DEPTH, D)),
        'w_in': nrm((DEPTH, D, N_IN), D ** -0.5),
        'conv_a_w': nrm((DEPTH, CONV_A, WA), CONV_A ** -0.5),
        'w_a_out': nrm((DEPTH, WA, D), WA ** -0.5),
        'q_norm_g': gain((DEPTH, Q_LORA)),
        'w_q_up': nrm((DEPTH, Q_LORA, MLA_HEADS * QK_DIM), Q_LORA ** -0.5),
        'kv_norm_g': gain((DEPTH, KV_LORA)),
        'w_kv_up': nrm((DEPTH, KV_LORA, MLA_HEADS * (NOPE_DIM + V_DIM)), KV_LORA ** -0.5),
        'q_head_g': gain((DEPTH, QK_DIM)),
        'k_head_g': gain((DEPTH, QK_DIM)),
        'w_mla_out': nrm((DEPTH, MLA_HEADS * V_DIM, D), (MLA_HEADS * V_DIM) ** -0.5),
        'w_pool': nrm((DEPTH, len(POOL_WINDOWS), POOL_GROUP, POOL_OUT), POOL_GROUP ** -0.5),
        'pool_scale': gain((DEPTH, D)),
        'conv_d_w': nrm((DEPTH, CONV_D, WD), CONV_D ** -0.5),
        'conv_d_b': nrm((DEPTH, WD), 0.02),
        'cd_ln_g': gain((DEPTH, WD)),
        'cd_ln_b': nrm((DEPTH, WD), 0.02),
        'w_d_out': nrm((DEPTH, WD, D), WD ** -0.5),
        'w_out': nrm((DEPTH, D, D), D ** -0.5),
        'norm2_g': gain((DEPTH, D)),
        'w_up': nrm((DEPTH, D, 2 * D_FF), D ** -0.5),
        'conv_ff_w': nrm((DEPTH, CONV_FF, D_FF), CONV_FF ** -0.5),
        'conv_ff_b': nrm((DEPTH, D_FF), 0.02),
        'w_down': nrm((DEPTH, D_FF, D), D_FF ** -0.5),
    }


def reference(x, c, ctx, c_ctx, ada_w, ada_b, norm1_g, w_in, conv_a_w, w_a_out, q_norm_g, w_q_up,
              kv_norm_g, w_kv_up, q_head_g, k_head_g, w_mla_out, w_pool, pool_scale, conv_d_w,
              conv_d_b, cd_ln_g, cd_ln_b, w_d_out, w_out, norm2_g, w_up, conv_ff_w, conv_ff_b, w_down):
    B, L, _ = x.shape
    n_ctx = ctx.shape[1]
    rope = axial_rope_tables(L, x.dtype)
    for l in range(DEPTH):
        last = l == DEPTH - 1
        mod_x = (jax.nn.silu(c) @ ada_w[l] + ada_b[l])[:, None, :]
        mod_c = (jax.nn.silu(c_ctx) @ ada_w[l] + ada_b[l])[None, None, :]
        sh1, sc1, g1, sh2, sc2, g2 = jnp.split(mod_x, 6, axis=-1)
        csh1, csc1, cg1, csh2, csc2, cg2 = jnp.split(mod_c, 6, axis=-1)

        hx = modulate(rms_norm(x, norm1_g[l]), sh1, sc1)
        hc = modulate(rms_norm(ctx, norm1_g[l]), csh1, csc1)
        zx = jnp.split(hx @ w_in[l], IN_SPLITS, axis=-1)
        if last:
            zkvr = hc @ w_in[l][:, OFF_KV:OFF_P]
            zc_kv, zc_kr = zkvr[..., :KV_LORA], zkvr[..., KV_LORA:]
        else:
            zc = jnp.split(hc @ w_in[l], IN_SPLITS, axis=-1)
            zc_kv, zc_kr = zc[2], zc[3]

        k_c, v_c = mla_kv(zc_kv, zc_kr, kv_norm_g[l], w_kv_up[l], k_head_g[l], None)
        k_x, v_x = mla_kv(zx[2], zx[3], kv_norm_g[l], w_kv_up[l], k_head_g[l], rope)
        q_x = mla_q(zx[1], q_norm_g[l], w_q_up[l], q_head_g[l], rope)
        k_all = jnp.concatenate([k_c, k_x], axis=1)
        v_all = jnp.concatenate([v_c, v_x], axis=1)
        att_x = latent_attention(q_x, k_all, v_all) @ w_mla_out[l]

        ys_x = (short_conv(zx[0], conv_a_w[l], w_a_out[l]),
                att_x,
                pool_mixer(zx[4], w_pool[l], pool_scale[l]),
                conformer_conv(zx[5], conv_d_w[l], conv_d_b[l], cd_ln_g[l], cd_ln_b[l], w_d_out[l]))
        x_new = x + g1 * gated_merge(zx[6], ys_x, w_out[l])
        hx2 = modulate(rms_norm(x_new, norm2_g[l]), sh2, sc2)
        x_new = x_new + g2 * conv_ffn(hx2, w_up[l], conv_ff_w[l], conv_ff_b[l], w_down[l])

        if not last:
            q_c = mla_q(zc[1], q_norm_g[l], w_q_up[l], q_head_g[l], None)
            att_c = softmax_attention(q_c, k_c, v_c).reshape(B, n_ctx, MLA_HEADS * V_DIM) @ w_mla_out[l]
            ys_c = (short_conv(zc[0], conv_a_w[l], w_a_out[l]),
                    att_c,
                    pool_mixer(zc[4], w_pool[l], pool_scale[l]),
                    conformer_conv(zc[5], conv_d_w[l], conv_d_b[l], cd_ln_g[l], cd_ln_b[l], w_d_out[l]))
            ctx = ctx + cg1 * gated_merge(zc[6], ys_c, w_out[l])
            hc2 = modulate(rms_norm(ctx, norm2_g[l]), csh2, csc2)
            ctx = ctx + cg2 * conv_ffn(hc2, w_up[l], conv_ff_w[l], conv_ff_b[l], w_down[l])
        x = x_new
    return x
```

```python
import functools

import jax
import jax.numpy as jnp
from jax import lax
from jax.experimental import pallas as pl
from jax.experimental.pallas import tpu as pltpu

GRID_W = 64
ROPE_THETA = 10000.0
POOL_WINDOWS = (2, 4, 8, 16)
EPS = 1e-6
LN_EPS = 1e-5

LANE = 128
HALO = 16
VMEM_LIMIT = 56 * 1024 * 1024

F32 = jnp.float32
BF16 = jnp.bfloat16


def _cparams(*sem):
    return pltpu.CompilerParams(dimension_semantics=sem, vmem_limit_bytes=VMEM_LIMIT)


def _pick(n, pref):
    if n <= pref:
        return n
    t = pref
    while n % t:
        t //= 2
    return t


def _ada_body(cc_ref, w_ref, b_ref, o_ref):
    cc = cc_ref[...]
    s = (cc * jax.nn.sigmoid(cc)).astype(BF16)
    o_ref[...] = jnp.dot(s, w_ref[...].astype(BF16), preferred_element_type=F32) + b_ref[...]


def _ada(cc, ada_w, ada_b):
    depth, d, n = ada_w.shape
    rows = cc.shape[0]
    tn = _pick(n, 1024)
    return pl.pallas_call(
        _ada_body,
        grid=(depth, n // tn),
        in_specs=[
            pl.BlockSpec((rows, d), lambda l, j: (0, 0)),
            pl.BlockSpec((None, d, tn), lambda l, j: (l, 0, j)),
            pl.BlockSpec((None, 1, tn), lambda l, j: (l, 0, j)),
        ],
        out_specs=pl.BlockSpec((None, rows, tn), lambda l, j: (l, 0, j)),
        out_shape=jax.ShapeDtypeStruct((depth, rows, n), F32),
        compiler_params=_cparams("arbitrary", "arbitrary"),
        name="ada",
    )(cc, ada_w, ada_b.reshape(depth, 1, n))


def _norm_mm_body(x_ref, g_ref, sh_ref, sc_ref, w_ref, z_ref, h_ref):
    @pl.when(pl.program_id(1) == 0)
    def _():
        x = x_ref[...]
        ms = jnp.mean(x * x, axis=-1, keepdims=True)
        y = x * lax.rsqrt(ms + EPS) * g_ref[...]
        h_ref[...] = (y * (1.0 + sc_ref[...]) + sh_ref[...]).astype(h_ref.dtype)

    z_ref[...] = jnp.dot(h_ref[...], w_ref[...], preferred_element_type=F32).astype(z_ref.dtype)


def _norm_mm(x, gain, mods, k_shift, rows_per_mod, w, *, tn, emit_h):
    m, d = x.shape
    n = w.shape[1]
    tm = _pick(rows_per_mod, 1024)
    tn = _pick(n, tn)
    in_specs = [
        pl.BlockSpec((tm, d), lambda i, j: (i, 0)),
        pl.BlockSpec((1, d), lambda i, j: (0, 0)),
        pl.BlockSpec((None, 1, d), lambda i, j: (i * tm // rows_per_mod, 0, k_shift)),
        pl.BlockSpec((None, 1, d), lambda i, j: (i * tm // rows_per_mod, 0, k_shift + 1)),
        pl.BlockSpec((d, tn), lambda i, j: (0, j)),
    ]
    z_spec = pl.BlockSpec((tm, tn), lambda i, j: (i, j))
    z_shape = jax.ShapeDtypeStruct((m, n), BF16)
    if emit_h:
        out_specs = [z_spec, pl.BlockSpec((tm, d), lambda i, j: (i, 0))]
        out_shape = [z_shape, jax.ShapeDtypeStruct((m, d), BF16)]
        scratch = []
    else:
        out_specs, out_shape = z_spec, z_shape
        scratch = [pltpu.VMEM((tm, d), BF16)]
    return pl.pallas_call(
        _norm_mm_body,
        grid=(m // tm, n // tn),
        in_specs=in_specs,
        out_specs=out_specs,
        out_shape=out_shape,
        scratch_shapes=scratch,
        compiler_params=_cparams("arbitrary", "arbitrary"),
        name="norm_mm_h" if emit_h else "norm_mm",
    )(x, gain.reshape(1, d), mods, mods, w)


def _mm_res_body(a_ref, w_ref, r_ref, gate_ref, o_ref):
    acc = jnp.dot(a_ref[...], w_ref[...], preferred_element_type=F32)
    o_ref[...] = r_ref[...] + gate_ref[...] * acc


def _mm_res(a, w, res, mods, k_gate, rows_per_mod, *, tm, tn):
    m, k = a.shape
    n = w.shape[1]
    tm = _pick(rows_per_mod, tm)
    tn = _pick(n, tn)
    nj = n // tn
    return pl.pallas_call(
        _mm_res_body,
        grid=(m // tm, nj),
        in_specs=[
            pl.BlockSpec((tm, k), lambda i, j: (i, 0)),
            pl.BlockSpec((k, tn), lambda i, j: (0, j)),
            pl.BlockSpec((tm, tn), lambda i, j: (i, j)),
            pl.BlockSpec((None, 1, tn), lambda i, j: (i * tm // rows_per_mod, 0, k_gate * nj + j)),
        ],
        out_specs=pl.BlockSpec((tm, tn), lambda i, j: (i, j)),
        out_shape=jax.ShapeDtypeStruct((m, n), F32),
        compiler_params=_cparams("arbitrary", "arbitrary"),
        name="mm_res",
    )(a, w, res, mods)


def _halo_specs(tl, width, col_block, n_rows):
    per = tl // HALO
    last = n_rows // HALO - 1
    prev = pl.BlockSpec((HALO, width), lambda i, *_: (jnp.maximum(i * per - 1, 0), col_block))
    nxt = pl.BlockSpec((HALO, width), lambda i, *_: (jnp.minimum((i + 1) * per, last), col_block))
    return prev, nxt


def _fill_ext(ext_ref, prev, main, nxt, first, last, tl):
    ext_ref[0:HALO, :] = jnp.where(first, 0.0, prev)
    ext_ref[HALO:HALO + tl, :] = main
    ext_ref[HALO + tl:HALO + tl + HALO, :] = jnp.where(last, 0.0, nxt)


def _seq_edges(tps):
    s = pl.program_id(0) % tps
    return s, s == 0, s == tps - 1


def _conv_a_body(main_ref, b_ref, prev_ref, next_ref, w_ref, o_ref, ext_ref, *, tl, tps, width):
    _, first, last = _seq_edges(tps)

    def prod(blk):
        blk = blk.astype(F32)
        return blk[:, :width] * blk[:, width:]

    _fill_ext(ext_ref, prod(prev_ref[...]), prod(main_ref[...]), prod(next_ref[...]), first, last, tl)
    taps = w_ref.shape[0]
    pad = taps // 2
    acc = w_ref[0:1, :] * ext_ref[HALO - pad:HALO - pad + tl, :]
    for k in range(1, taps):
        acc = acc + w_ref[k:k + 1, :] * ext_ref[HALO - pad + k:HALO - pad + k + tl, :]
    o_ref[...] = (b_ref[...].astype(F32) * acc).astype(o_ref.dtype)


def _conv_a(z, conv_w, width, seq_len):
    m = z.shape[0]
    tl = _pick(seq_len, 256)
    prev, nxt = _halo_specs(tl, 2 * width, 1, m)
    return pl.pallas_call(
        functools.partial(_conv_a_body, tl=tl, tps=seq_len // tl, width=width),
        grid=(m // tl,),
        in_specs=[
            pl.BlockSpec((tl, 2 * width), lambda i: (i, 1)),
            pl.BlockSpec((tl, width), lambda i: (i, 4)),
            prev, nxt,
            pl.BlockSpec(conv_w.shape, lambda i: (0, 0)),
        ],
        out_specs=pl.BlockSpec((tl, width), lambda i: (i, 0)),
        out_shape=jax.ShapeDtypeStruct((m, width), BF16),
        scratch_shapes=[pltpu.VMEM((tl + 2 * HALO, width), F32)],
        compiler_params=_cparams("arbitrary"),
        name="conv_a",
    )(z, z, z, z, conv_w)


def _pool_body(main_ref, prev_ref, next_ref, o_ref, ext_ref, *, tl, tps, seq_len, group):
    s, first, last = _seq_edges(tps)
    _fill_ext(ext_ref, prev_ref[...].astype(F32), main_ref[...].astype(F32), next_ref[...].astype(F32),
              first, last, tl)
    pos = s * tl + lax.broadcasted_iota(jnp.int32, (tl, 1), 0)
    for g, win in enumerate(POOL_WINDOWS):
        cols = slice(g * group, (g + 1) * group)
        start = -(win // 2)
        acc = ext_ref[HALO + start:HALO + start + tl, cols]
        for j in range(start + 1, start + win):
            acc = acc + ext_ref[HALO + j:HALO + j + tl, cols]
        lo = jnp.maximum(pos + start, 0)
        hi = jnp.minimum(pos + start + win, seq_len)
        cnt = (hi - lo).astype(F32)
        o_ref[:, cols] = (acc / cnt - ext_ref[HALO:HALO + tl, cols]).astype(o_ref.dtype)


def _pool(z, width, seq_len):
    m = z.shape[0]
    tl = _pick(seq_len, 256)
    prev, nxt = _halo_specs(tl, width, 5, m)
    return pl.pallas_call(
        functools.partial(_pool_body, tl=tl, tps=seq_len // tl, seq_len=seq_len,
                          group=width // len(POOL_WINDOWS)),
        grid=(m // tl,),
        in_specs=[pl.BlockSpec((tl, width), lambda i: (i, 5)), prev, nxt],
        out_specs=pl.BlockSpec((tl, width), lambda i: (i, 0)),
        out_shape=jax.ShapeDtypeStruct((m, width), BF16),
        scratch_shapes=[pltpu.VMEM((tl + 2 * HALO, width), F32)],
        compiler_params=_cparams("arbitrary"),
        name="pool",
    )(z, z, z)


def _conv_d_body(main_ref, prev_ref, next_ref, w_ref, cb_ref, lg_ref, lb_ref, o_ref, ext_ref,
                 *, tl, tps, width):
    _, first, last = _seq_edges(tps)

    def glu(blk):
        blk = blk.astype(F32)
        return blk[:, :width] * jax.nn.sigmoid(blk[:, width:])

    _fill_ext(ext_ref, glu(prev_ref[...]), glu(main_ref[...]), glu(next_ref[...]), first, last, tl)
    taps = w_ref.shape[0]
    pad = taps // 2
    acc = w_ref[0:1, :] * ext_ref[HALO - pad:HALO - pad + tl, :]
    for k in range(1, taps):
        acc = acc + w_ref[k:k + 1, :] * ext_ref[HALO - pad + k:HALO - pad + k + tl, :]
    y = acc + cb_ref[...]
    mu = jnp.mean(y, axis=-1, keepdims=True)
    yc = y - mu
    var = jnp.mean(yc * yc, axis=-1, keepdims=True)
    yn = yc * lax.rsqrt(var + LN_EPS) * lg_ref[...] + lb_ref[...]
    o_ref[...] = (yn * jax.nn.sigmoid(yn)).astype(o_ref.dtype)


def _conv_d(z, conv_w, conv_b, ln_g, ln_b, width, seq_len):
    m = z.shape[0]
    tl = _pick(seq_len, 256)
    prev, nxt = _halo_specs(tl, 2 * width, 0, m)
    vec = pl.BlockSpec((1, width), lambda i: (0, 0))
    return pl.pallas_call(
        functools.partial(_conv_d_body, tl=tl, tps=seq_len // tl, width=width),
        grid=(m // tl,),
        in_specs=[
            pl.BlockSpec((tl, 2 * width), lambda i: (i, 0)),
            prev, nxt,
            pl.BlockSpec(conv_w.shape, lambda i: (0, 0)),
            vec, vec, vec,
        ],
        out_specs=pl.BlockSpec((tl, width), lambda i: (i, 0)),
        out_shape=jax.ShapeDtypeStruct((m, width), BF16),
        scratch_shapes=[pltpu.VMEM((tl + 2 * HALO, width), F32)],
        compiler_params=_cparams("arbitrary"),
        name="conv_d",
    )(z, z, z, conv_w, conv_b.reshape(1, width), ln_g.reshape(1, width), ln_b.reshape(1, width))


def _ffn_mid_body(gate_ref, val_ref, prev_ref, next_ref, w_ref, cb_ref, o_ref, ext_ref, *, tl, tps):
    _, first, last = _seq_edges(tps)
    _fill_ext(ext_ref, prev_ref[...].astype(F32), gate_ref[...].astype(F32), next_ref[...].astype(F32),
              first, last, tl)
    taps = w_ref.shape[0]
    pad = taps // 2
    acc = w_ref[0:1, :] * ext_ref[HALO - pad:HALO - pad + tl, :]
    for k in range(1, taps):
        acc = acc + w_ref[k:k + 1, :] * ext_ref[HALO - pad + k:HALO - pad + k + tl, :]
    g = acc + cb_ref[...]
    o_ref[...] = (g * jax.nn.sigmoid(g) * val_ref[...].astype(F32)).astype(o_ref.dtype)


def _ffn_mid(u, conv_w, conv_b, seq_len):
    m = u.shape[0]
    d_ff = conv_w.shape[1]
    tl = _pick(seq_len, 256)
    tc = _pick(d_ff, 512)
    nc = d_ff // tc
    per = tl // HALO
    last_blk = m // HALO - 1
    return pl.pallas_call(
        functools.partial(_ffn_mid_body, tl=tl, tps=seq_len // tl),
        grid=(m // tl, nc),
        in_specs=[
            pl.BlockSpec((tl, tc), lambda i, c: (i, c)),
            pl.BlockSpec((tl, tc), lambda i, c: (i, nc + c)),
            pl.BlockSpec((HALO, tc), lambda i, c: (jnp.maximum(i * per - 1, 0), c)),
            pl.BlockSpec((HALO, tc), lambda i, c: (jnp.minimum((i + 1) * per, last_blk), c)),
            pl.BlockSpec((conv_w.shape[0], tc), lambda i, c: (0, c)),
            pl.BlockSpec((1, tc), lambda i, c: (0, c)),
        ],
        out_specs=pl.BlockSpec((tl, tc), lambda i, c: (i, c)),
        out_shape=jax.ShapeDtypeStruct((m, d_ff), BF16),
        scratch_shapes=[pltpu.VMEM((tl + 2 * HALO, tc), F32)],
        compiler_params=_cparams("arbitrary", "arbitrary"),
        name="ffn_mid",
    )(u, u, u, u, conv_w, conv_b.reshape(1, d_ff))


def _rope_vreg(bn, cos_ref, sin_ref):
    return bn * cos_ref[...] + pltpu.roll(bn, LANE // 2, 1) * sin_ref[...]


def _qkv_prep_body(z_ref, qg_ref, wq_ref, kvg_ref, wkv_ref, qhg_ref, khg_ref, cos_ref, sin_ref,
                   q_ref, k_ref, v_ref, *, q_lora, kv_lora, heads, qk_dim, scale):
    z = z_ref[...].astype(F32)
    lane = lax.broadcasted_iota(jnp.int32, (1, LANE), 1)
    rope_lanes = lane < (LANE // 2)
    slot = 2 * LANE

    def rms(v, g):
        return v * lax.rsqrt(jnp.mean(v * v, axis=-1, keepdims=True) + EPS) * g

    qn = rms(z[:, :q_lora], qg_ref[...]).astype(BF16)
    yq = jnp.dot(qn, wq_ref[...], preferred_element_type=F32)
    g_nope, g_rope = qhg_ref[:, :LANE], qhg_ref[:, LANE:]
    for h in range(heads):
        a = yq[:, h * slot:h * slot + LANE]
        b = yq[:, h * slot + LANE:(h + 1) * slot]
        ss = jnp.sum(a * a + jnp.where(rope_lanes, b * b, 0.0), axis=-1, keepdims=True)
        r = lax.rsqrt(ss / qk_dim + EPS) * scale
        q_ref[:, h * slot:h * slot + LANE] = (a * r * g_nope).astype(q_ref.dtype)
        q_ref[:, h * slot + LANE:(h + 1) * slot] = _rope_vreg(b * r * g_rope, cos_ref, sin_ref).astype(q_ref.dtype)

    kvn = rms(z[:, q_lora:q_lora + kv_lora], kvg_ref[...]).astype(BF16)
    ykv = jnp.dot(kvn, wkv_ref[...], preferred_element_type=F32)
    kr = z[:, q_lora + kv_lora:q_lora + kv_lora + LANE]
    kr_ss = jnp.sum(jnp.where(rope_lanes, kr * kr, 0.0), axis=-1, keepdims=True)
    g_nope, g_rope = khg_ref[:, :LANE], khg_ref[:, LANE:]
    for h in range(heads):
        a = ykv[:, h * LANE:(h + 1) * LANE]
        ss = jnp.sum(a * a, axis=-1, keepdims=True) + kr_ss
        r = lax.rsqrt(ss / qk_dim + EPS)
        k_ref[:, h * slot:h * slot + LANE] = (a * r * g_nope).astype(k_ref.dtype)
        k_ref[:, h * slot + LANE:(h + 1) * slot] = _rope_vreg(kr * r * g_rope, cos_ref, sin_ref).astype(k_ref.dtype)
    v_ref[...] = ykv[:, heads * LANE:].astype(v_ref.dtype)


def _qkv_prep(z, lw, cos_t, sin_t, seq_len, dims, use_rope):
    m = z.shape[0]
    heads, q_lora, kv_lora, qk_dim = dims["heads"], dims["q_lora"], dims["kv_lora"], dims["qk_dim"]
    nq = dims["nq"]
    tm = _pick(seq_len, 512)
    tps = seq_len // tm
    col_block = (z.shape[1] - nq) // nq
    if use_rope:
        tab = pl.BlockSpec((tm, LANE), lambda i: (i % tps, 0))
    else:
        tab = pl.BlockSpec((tm, LANE), lambda i: (0, 0))

    def full(a):
        return pl.BlockSpec(a.shape, lambda i: (0, 0))

    qw = heads * 2 * LANE
    return pl.pallas_call(
        functools.partial(_qkv_prep_body, q_lora=q_lora, kv_lora=kv_lora, heads=heads, qk_dim=qk_dim,
                          scale=qk_dim ** -0.5),
        grid=(m // tm,),
        in_specs=[
            pl.BlockSpec((tm, nq), lambda i: (i, col_block)),
            full(lw["q_norm_g"]), full(lw["wq"]), full(lw["kv_norm_g"]), full(lw["wkv"]),
            full(lw["q_head_g"]), full(lw["k_head_g"]), tab, tab,
        ],
        out_specs=[
            pl.BlockSpec((tm, qw), lambda i: (i, 0)),
            pl.BlockSpec((tm, qw), lambda i: (i, 0)),
            pl.BlockSpec((tm, heads * LANE), lambda i: (i, 0)),
        ],
        out_shape=[
            jax.ShapeDtypeStruct((m, qw), BF16),
            jax.ShapeDtypeStruct((m, qw), BF16),
            jax.ShapeDtypeStruct((m, heads * LANE), BF16),
        ],
        compiler_params=_cparams("arbitrary"),
        name="qkv_prep",
    )(z, lw["q_norm_g"], lw["wq"], lw["kv_norm_g"], lw["wkv"], lw["q_head_g"], lw["k_head_g"], cos_t, sin_t)


def _attn_update(q, k, v, carry):
    m_i, l_i, acc = carry
    s = lax.dot_general(q, k, (((1,), (1,)), ((), ())), preferred_element_type=F32)
    m_new = jnp.maximum(m_i, jnp.max(s, axis=-1, keepdims=True))
    alpha = jnp.exp(m_i - m_new)
    p = jnp.exp(s - m_new)
    l_new = alpha * l_i + jnp.sum(p, axis=-1, keepdims=True)
    acc = alpha * acc + jnp.dot(p.astype(v.dtype), v, preferred_element_type=F32)
    return m_new, l_new, acc


def _attn_body(*refs, tk, n_lat):
    if n_lat:
        q_ref, kc_ref, vc_ref, kx_ref, vx_ref, o_ref = refs
    else:
        q_ref, kc_ref, vc_ref, o_ref = refs
    q = q_ref[...]
    tq = q.shape[0]
    carry = (jnp.full((tq, 1), -jnp.inf, F32), jnp.zeros((tq, 1), F32), jnp.zeros((tq, o_ref.shape[-1]), F32))
    carry = _attn_update(q, kc_ref[...], vc_ref[...], carry)
    if n_lat:
        def step(c, carry):
            off = pl.multiple_of(c * tk, tk)
            return _attn_update(q, kx_ref[pl.ds(off, tk), :], vx_ref[pl.ds(off, tk), :], carry)
        carry = lax.fori_loop(0, n_lat, step, carry)
    _, l_i, acc = carry
    o_ref[...] = (acc / l_i).astype(o_ref.dtype)


def _attention(q, kc, vc, kx, vx, batch, heads):
    lq = q.shape[0] // batch
    lc = kc.shape[0] // batch
    slot, vd = 2 * LANE, LANE
    tq = _pick(lq, 512)
    q3 = q.reshape(batch, lq, heads * slot)
    args = [q3, kc.reshape(batch, lc, heads * slot), vc.reshape(batch, lc, heads * vd)]
    in_specs = [
        pl.BlockSpec((None, tq, slot), lambda b, h, i: (b, i, h)),
        pl.BlockSpec((None, lc, slot), lambda b, h, i: (b, 0, h)),
        pl.BlockSpec((None, lc, vd), lambda b, h, i: (b, 0, h)),
    ]
    n_lat, tk = 0, 0
    if kx is not None:
        lx = kx.shape[0] // batch
        tk = _pick(lx, 512)
        n_lat = lx // tk
        args += [kx.reshape(batch, lx, heads * slot), vx.reshape(batch, lx, heads * vd)]
        in_specs += [
            pl.BlockSpec((None, lx, slot), lambda b, h, i: (b, 0, h)),
            pl.BlockSpec((None, lx, vd), lambda b, h, i: (b, 0, h)),
        ]
    out = pl.pallas_call(
        functools.partial(_attn_body, tk=tk, n_lat=n_lat),
        grid=(batch, heads, lq // tq),
        in_specs=in_specs,
        out_specs=pl.BlockSpec((None, tq, vd), lambda b, h, i: (b, i, h)),
        out_shape=jax.ShapeDtypeStruct((batch, lq, heads * vd), BF16),
        compiler_params=_cparams("arbitrary", "arbitrary", "arbitrary"),
        name="attention_x" if n_lat else "attention_c",
    )(*args)
    return out.reshape(batch * lq, heads * vd)


def _merge_body(h_ref, a_ref, t_ref, p_ref, d_ref, wg_ref, wa_ref, wt_ref, wp_ref, ps_ref, wd_ref, o_ref):
    h = h_ref[...]

    def gate(i):
        return jax.nn.sigmoid(jnp.dot(h, wg_ref[i], preferred_element_type=F32))

    def proj(x_ref, w):
        return jnp.dot(x_ref[...], w, preferred_element_type=F32)

    acc = gate(0) * proj(a_ref, wa_ref[...])
    acc = acc + gate(1) * proj(t_ref, wt_ref[...])
    acc = acc + gate(2) * (proj(p_ref, wp_ref[...]) * ps_ref[...])
    acc = acc + gate(3) * proj(d_ref, wd_ref[...])
    o_ref[...] = acc.astype(o_ref.dtype)


def _merge(h, act_a, att, act_p, act_d, lw, rows_per_seq):
    m, d = h.shape
    n_branch = lw["wg"].shape[0]
    tn = d // n_branch
    group = act_p.shape[1] // n_branch
    tm = _pick(rows_per_seq, 512)

    def act(a):
        return pl.BlockSpec((tm, a.shape[1]), lambda j, i: (i, 0))

    def wcol(w):
        return pl.BlockSpec((w.shape[0], tn), lambda j, i: (0, j))

    return pl.pallas_call(
        _merge_body,
        grid=(n_branch, m // tm),
        in_specs=[
            act(h), act(act_a), act(att),
            pl.BlockSpec((tm, group), lambda j, i: (i, j)),
            act(act_d),
            pl.BlockSpec((n_branch, d, tn), lambda j, i: (0, 0, j)),
            wcol(lw["w_a_out"]), wcol(lw["w_mla_out"]),
            pl.BlockSpec((None, group, tn), lambda j, i: (j, 0, 0)),
            pl.BlockSpec((1, tn), lambda j, i: (0, j)),
            wcol(lw["w_d_out"]),
        ],
        out_specs=pl.BlockSpec((tm, tn), lambda j, i: (i, j)),
        out_shape=jax.ShapeDtypeStruct((m, d), BF16),
        compiler_params=_cparams("arbitrary", "arbitrary"),
        name="merge",
    )(h, act_a, att, act_p, act_d, lw["wg"], lw["w_a_out"], lw["w_mla_out"], lw["w_pool"],
      lw["pool_scale"], lw["w_d_out"])


def _rope_perm(rope_dim):
    nf = rope_dim // 4
    j = jnp.arange(rope_dim)
    return jnp.where((j % (2 * nf)) < nf, j + nf, j - nf)


def _rope_tables(seq_len, rope_dim):
    rows = seq_len // GRID_W
    row = jnp.broadcast_to(jnp.arange(rows)[:, None], (rows, GRID_W)).reshape(seq_len)
    col = jnp.broadcast_to(jnp.arange(GRID_W)[None, :], (rows, GRID_W)).reshape(seq_len)
    nf = rope_dim // 4
    inv = ROPE_THETA ** (-jnp.arange(nf, dtype=F32) / nf)
    ang = jnp.stack([row, col], axis=-1).astype(F32)[:, :, None] * inv
    cos, sin = jnp.cos(ang), jnp.sin(ang)
    cos_t = jnp.concatenate([cos, cos], axis=-1).reshape(seq_len, rope_dim)
    sin_t = jnp.concatenate([-sin, sin], axis=-1).reshape(seq_len, rope_dim)
    pad = jnp.zeros((seq_len, LANE - rope_dim), F32)
    return jnp.concatenate([cos_t, pad], axis=-1), jnp.concatenate([sin_t, pad], axis=-1)


def _layer_weights(l, p, dims):
    d, wa, wc, wd = dims["d"], dims["wa"], dims["wc"], dims["wd"]
    heads, nope, rope, vdim = dims["heads"], dims["nope"], dims["rope"], dims["vdim"]
    q_lora, kv_lora, nq = dims["q_lora"], dims["kv_lora"], dims["nq"]
    perm = _rope_perm(rope)
    w_in = p["w_in"][l]
    off_q = 3 * wa
    off_kv = off_q + q_lora
    off_kr = off_kv + kv_lora
    off_p = off_kr + rope
    off_d = off_p + wc
    off_g = off_d + 2 * wd
    kr = w_in[:, off_kr:off_p]
    qkv_used = q_lora + kv_lora + 2 * rope
    w_z = jnp.concatenate([
        w_in[:, off_d:off_g],
        w_in[:, wa:3 * wa],
        w_in[:, :wa],
        w_in[:, off_p:off_d],
        w_in[:, off_q:off_kr],
        kr, kr[:, perm],
        jnp.zeros((d, nq - qkv_used), w_in.dtype),
    ], axis=1).astype(BF16)
    n_branch = (w_in.shape[1] - off_g) // d
    wg = w_in[:, off_g:].reshape(d, n_branch, d).transpose(1, 0, 2).astype(BF16)

    qk = nope + rope
    wq = p["w_q_up"][l].reshape(q_lora, heads, qk)
    wq = jnp.concatenate([wq, wq[:, :, nope:][:, :, perm]], axis=-1).reshape(q_lora, heads * 2 * LANE)
    wkv = p["w_kv_up"][l].reshape(kv_lora, heads, nope + vdim)
    wkv = jnp.concatenate([wkv[:, :, :nope].reshape(kv_lora, heads * nope),
                           wkv[:, :, nope:].reshape(kv_lora, heads * vdim)], axis=1)

    def head_gain(g):
        return jnp.concatenate([g, g[nope:][perm]]).reshape(1, 2 * LANE)

    return {
        "w_z": w_z, "wg": wg,
        "wq": wq.astype(BF16), "wkv": wkv.astype(BF16),
        "q_norm_g": p["q_norm_g"][l].reshape(1, q_lora), "kv_norm_g": p["kv_norm_g"][l].reshape(1, kv_lora),
        "q_head_g": head_gain(p["q_head_g"][l]), "k_head_g": head_gain(p["k_head_g"][l]),
        "w_a_out": p["w_a_out"][l].astype(BF16), "w_mla_out": p["w_mla_out"][l].astype(BF16),
        "w_pool": p["w_pool"][l].astype(BF16), "pool_scale": p["pool_scale"][l].reshape(1, d),
        "w_d_out": p["w_d_out"][l].astype(BF16), "w_out": p["w_out"][l].astype(BF16),
        "w_up": p["w_up"][l].astype(BF16), "w_down": p["w_down"][l].astype(BF16),
    }


def _token_mixer_inputs(xs, mods, rows_per_mod, seq_len, l, p, lw, dims, tables):
    z, h = _norm_mm(xs, p["norm1_g"][l], mods, 0, rows_per_mod, lw["w_z"], tn=1536, emit_h=True)
    q, k, v = _qkv_prep(z, lw, tables[0], tables[1], seq_len, dims, tables[2])
    return z, h, q, k, v


def _finish_layer(xs, mods, rows_per_mod, seq_len, l, p, lw, dims, z, h, att):
    wa, wc, wd = dims["wa"], dims["wc"], dims["wd"]
    act_a = _conv_a(z, p["conv_a_w"][l], wa, seq_len)
    act_p = _pool(z, wc, seq_len)
    act_d = _conv_d(z, p["conv_d_w"][l], p["conv_d_b"][l], p["cd_ln_g"][l], p["cd_ln_b"][l], wd, seq_len)
    merged = _merge(h, act_a, att, act_p, act_d, lw, seq_len)
    x1 = _mm_res(merged, lw["w_out"], xs, mods, 2, rows_per_mod, tm=1024, tn=1024)
    u = _norm_mm(x1, p["norm2_g"][l], mods, 3, rows_per_mod, lw["w_up"], tn=1024, emit_h=False)
    f = _ffn_mid(u, p["conv_ff_w"][l], p["conv_ff_b"][l], seq_len)
    return _mm_res(f, lw["w_down"], x1, mods, 5, rows_per_mod, tm=512, tn=1024)


def kernel(x, c, ctx, c_ctx, ada_w, ada_b, norm1_g, w_in, conv_a_w, w_a_out, q_norm_g, w_q_up, kv_norm_g, w_kv_up, q_head_g, k_head_g, w_mla_out, w_pool, pool_scale, conv_d_w, conv_d_b, cd_ln_g, cd_ln_b, w_d_out, w_out, norm2_g, w_up, conv_ff_w, conv_ff_b, w_down):
    p = dict(norm1_g=norm1_g, w_in=w_in, conv_a_w=conv_a_w, w_a_out=w_a_out, q_norm_g=q_norm_g, w_q_up=w_q_up,
             kv_norm_g=kv_norm_g, w_kv_up=w_kv_up, q_head_g=q_head_g, k_head_g=k_head_g, w_mla_out=w_mla_out,
             w_pool=w_pool, pool_scale=pool_scale, conv_d_w=conv_d_w, conv_d_b=conv_d_b, cd_ln_g=cd_ln_g,
             cd_ln_b=cd_ln_b, w_d_out=w_d_out, w_out=w_out, norm2_g=norm2_g, w_up=w_up, conv_ff_w=conv_ff_w,
             conv_ff_b=conv_ff_b, w_down=w_down)
    batch, seq, d = x.shape
    n_ctx = ctx.shape[1]
    depth = ada_w.shape[0]
    qk_dim = q_head_g.shape[1]
    heads = w_q_up.shape[2] // qk_dim
    vdim = w_mla_out.shape[1] // heads
    nope = w_kv_up.shape[2] // heads - vdim
    rope = qk_dim - nope
    wa, wc, wd = conv_a_w.shape[2], w_pool.shape[1] * w_pool.shape[2], conv_d_w.shape[2]
    q_lora, kv_lora = w_q_up.shape[1], w_kv_up.shape[1]
    assert nope == LANE and vdim == LANE and 2 * rope == LANE, "head layout assumes 128 | 64 | 128 dims"
    assert wa == wc == wd and wa % (len(POOL_WINDOWS) * LANE) == 0
    nq = q_lora + kv_lora + 2 * rope
    while (6 * wa) % nq or nq % LANE:
        nq += LANE
    dims = dict(d=d, wa=wa, wc=wc, wd=wd, heads=heads, nope=nope, rope=rope, vdim=vdim, qk_dim=qk_dim,
                q_lora=q_lora, kv_lora=kv_lora, nq=nq)

    rows = -(-(batch + 1) // 8) * 8
    cc = jnp.zeros((rows, d), F32).at[:batch].set(c).at[batch].set(c_ctx)
    mods = _ada(cc, ada_w, ada_b)

    cos_t, sin_t = _rope_tables(seq, rope)
    ones_t = jnp.concatenate([jnp.ones((n_ctx, rope), F32), jnp.zeros((n_ctx, LANE - rope), F32)], axis=-1)
    zeros_t = jnp.zeros((n_ctx, LANE), F32)
    tab_x = (cos_t, sin_t, True)
    tab_c = (ones_t, zeros_t, False)

    xs = x.reshape(batch * seq, d)
    cs = ctx.reshape(batch * n_ctx, d)
    for l in range(depth):
        last = l == depth - 1
        lw = _layer_weights(l, p, dims)
        mods_x = mods[l, :batch].reshape(batch, 1, -1)
        mods_c = mods[l, batch:batch + 1].reshape(1, 1, -1)
        zc, hc, qc, kc, vc = _token_mixer_inputs(cs, mods_c, batch * n_ctx, n_ctx, l, p, lw, dims, tab_c)
        zx, hx, qx, kx, vx = _token_mixer_inputs(xs, mods_x, seq, seq, l, p, lw, dims, tab_x)
        att_x = _attention(qx, kc, vc, kx, vx, batch, heads)
        xs_new = _finish_layer(xs, mods_x, seq, seq, l, p, lw, dims, zx, hx, att_x)
        if not last:
            att_c = _attention(qc, kc, vc, None, None, batch, heads)
            cs = _finish_layer(cs, mods_c, batch * n_ctx, n_ctx, l, p, lw, dims, zc, hc, att_c)
        xs = xs_new
    return xs.reshape(batch, seq, d)
```

```python
import functools

import jax
import jax.numpy as jnp
from jax import lax
from jax.experimental import pallas as pl
from jax.experimental.pallas import tpu as pltpu

GRID_W = 64
ROPE_THETA = 10000.0
POOL_WINDOWS = (2, 4, 8, 16)
EPS = 1e-6
LN_EPS = 1e-5

LOG2E = 1.4426950408889634
LANE = 128
SUBLANES = 8
HALO = 16
VMEM_LIMIT = 56 * 1024 * 1024

F32 = jnp.float32
BF16 = jnp.bfloat16


def _cparams(*sem):
    return pltpu.CompilerParams(dimension_semantics=sem, vmem_limit_bytes=VMEM_LIMIT)


def _pick(n, pref):
    if n <= pref:
        return n
    t = pref
    while n % t:
        t //= 2
    return t


def _ada_body(cc_ref, w_ref, b_ref, o_ref):
    cc = cc_ref[...]
    s = (cc * jax.nn.sigmoid(cc)).astype(BF16)
    o_ref[...] = jnp.dot(s, w_ref[...].astype(BF16), preferred_element_type=F32) + b_ref[...]


def _ada(cc, ada_w, ada_b):
    depth, d, n = ada_w.shape
    rows = cc.shape[0]
    tn = _pick(n, 1024)
    return pl.pallas_call(
        _ada_body,
        grid=(depth, n // tn),
        in_specs=[
            pl.BlockSpec((rows, d), lambda l, j: (0, 0)),
            pl.BlockSpec((None, d, tn), lambda l, j: (l, 0, j)),
            pl.BlockSpec((None, 1, tn), lambda l, j: (l, 0, j)),
        ],
        out_specs=pl.BlockSpec((None, rows, tn), lambda l, j: (l, 0, j)),
        out_shape=jax.ShapeDtypeStruct((depth, rows, n), F32),
        compiler_params=_cparams("arbitrary", "arbitrary"),
        name="ada",
    )(cc, ada_w, ada_b.reshape(depth, 1, n))


def _norm_mm_body(x_ref, g_ref, sh_ref, sc_ref, w_ref, z_ref, h_ref):
    @pl.when(pl.program_id(1) == 0)
    def _():
        x = x_ref[...]
        ms = jnp.mean(x * x, axis=-1, keepdims=True)
        y = x * lax.rsqrt(ms + EPS) * g_ref[...]
        h_ref[...] = (y * (1.0 + sc_ref[...]) + sh_ref[...]).astype(h_ref.dtype)

    z_ref[...] = jnp.dot(h_ref[...], w_ref[...], preferred_element_type=F32).astype(z_ref.dtype)


def _norm_mm(x, gain, mods, k_shift, rows_per_mod, w, *, tn, emit_h):
    m, d = x.shape
    n = w.shape[1]
    tm = _pick(rows_per_mod, 1024)
    tn = _pick(n, tn)
    in_specs = [
        pl.BlockSpec((tm, d), lambda i, j: (i, 0)),
        pl.BlockSpec((1, d), lambda i, j: (0, 0)),
        pl.BlockSpec((None, 1, d), lambda i, j: (i * tm // rows_per_mod, 0, k_shift)),
        pl.BlockSpec((None, 1, d), lambda i, j: (i * tm // rows_per_mod, 0, k_shift + 1)),
        pl.BlockSpec((d, tn), lambda i, j: (0, j)),
    ]
    z_spec = pl.BlockSpec((tm, tn), lambda i, j: (i, j))
    z_shape = jax.ShapeDtypeStruct((m, n), BF16)
    if emit_h:
        out_specs = [z_spec, pl.BlockSpec((tm, d), lambda i, j: (i, 0))]
        out_shape = [z_shape, jax.ShapeDtypeStruct((m, d), BF16)]
        scratch = []
    else:
        out_specs, out_shape = z_spec, z_shape
        scratch = [pltpu.VMEM((tm, d), BF16)]
    return pl.pallas_call(
        _norm_mm_body,
        grid=(m // tm, n // tn),
        in_specs=in_specs,
        out_specs=out_specs,
        out_shape=out_shape,
        scratch_shapes=scratch,
        compiler_params=_cparams("arbitrary", "arbitrary"),
        name="norm_mm_h" if emit_h else "norm_mm",
    )(x, gain.reshape(1, d), mods, mods, w)


def _mm_res_body(a_ref, w_ref, r_ref, gate_ref, o_ref):
    acc = jnp.dot(a_ref[...], w_ref[...], preferred_element_type=F32)
    o_ref[...] = r_ref[...] + gate_ref[...] * acc


def _mm_res(a, w, res, mods, k_gate, rows_per_mod, *, tm, tn):
    m, k = a.shape
    n = w.shape[1]
    tm = _pick(rows_per_mod, tm)
    tn = _pick(n, tn)
    nj = n // tn
    return pl.pallas_call(
        _mm_res_body,
        grid=(m // tm, nj),
        in_specs=[
            pl.BlockSpec((tm, k), lambda i, j: (i, 0)),
            pl.BlockSpec((k, tn), lambda i, j: (0, j)),
            pl.BlockSpec((tm, tn), lambda i, j: (i, j)),
            pl.BlockSpec((None, 1, tn), lambda i, j: (i * tm // rows_per_mod, 0, k_gate * nj + j)),
        ],
        out_specs=pl.BlockSpec((tm, tn), lambda i, j: (i, j)),
        out_shape=jax.ShapeDtypeStruct((m, n), F32),
        compiler_params=_cparams("arbitrary", "arbitrary"),
        name="mm_res",
    )(a, w, res, mods)


def _halo_specs(tl, width, col_block, n_rows):
    per = tl // HALO
    last = n_rows // HALO - 1
    prev = pl.BlockSpec((HALO, width), lambda i, *_: (jnp.maximum(i * per - 1, 0), col_block))
    nxt = pl.BlockSpec((HALO, width), lambda i, *_: (jnp.minimum((i + 1) * per, last), col_block))
    return prev, nxt


def _fill_ext(ext_ref, prev, main, nxt, first, last, tl):
    ext_ref[0:HALO, :] = jnp.where(first, 0.0, prev)
    ext_ref[HALO:HALO + tl, :] = main
    ext_ref[HALO + tl:HALO + tl + HALO, :] = jnp.where(last, 0.0, nxt)


def _seq_edges(tps):
    s = pl.program_id(0) % tps
    return s, s == 0, s == tps - 1


def _conv_a_body(main_ref, b_ref, prev_ref, next_ref, w_ref, o_ref, ext_ref, *, tl, tps, width):
    _, first, last = _seq_edges(tps)

    def prod(blk):
        blk = blk.astype(F32)
        return blk[:, :width] * blk[:, width:]

    _fill_ext(ext_ref, prod(prev_ref[...]), prod(main_ref[...]), prod(next_ref[...]), first, last, tl)
    taps = w_ref.shape[0]
    pad = taps // 2
    acc = w_ref[0:1, :] * ext_ref[HALO - pad:HALO - pad + tl, :]
    for k in range(1, taps):
        acc = acc + w_ref[k:k + 1, :] * ext_ref[HALO - pad + k:HALO - pad + k + tl, :]
    o_ref[...] = (b_ref[...].astype(F32) * acc).astype(o_ref.dtype)


def _conv_a(z, conv_w, width, seq_len):
    m = z.shape[0]
    tl = _pick(seq_len, 256)
    prev, nxt = _halo_specs(tl, 2 * width, 1, m)
    return pl.pallas_call(
        functools.partial(_conv_a_body, tl=tl, tps=seq_len // tl, width=width),
        grid=(m // tl,),
        in_specs=[
            pl.BlockSpec((tl, 2 * width), lambda i: (i, 1)),
            pl.BlockSpec((tl, width), lambda i: (i, 4)),
            prev, nxt,
            pl.BlockSpec(conv_w.shape, lambda i: (0, 0)),
        ],
        out_specs=pl.BlockSpec((tl, width), lambda i: (i, 0)),
        out_shape=jax.ShapeDtypeStruct((m, width), BF16),
        scratch_shapes=[pltpu.VMEM((tl + 2 * HALO, width), F32)],
        compiler_params=_cparams("arbitrary"),
        name="conv_a",
    )(z, z, z, z, conv_w)


def _pool_body(main_ref, prev_ref, next_ref, o_ref, ext_ref, *, tl, tps, seq_len, group):
    s, first, last = _seq_edges(tps)
    _fill_ext(ext_ref, prev_ref[...].astype(F32), main_ref[...].astype(F32), next_ref[...].astype(F32),
              first, last, tl)
    pos = s * tl + lax.broadcasted_iota(jnp.int32, (tl, 1), 0)
    for g, win in enumerate(POOL_WINDOWS):
        cols = slice(g * group, (g + 1) * group)
        start = -(win // 2)
        acc = ext_ref[HALO + start:HALO + start + tl, cols]
        for j in range(start + 1, start + win):
            acc = acc + ext_ref[HALO + j:HALO + j + tl, cols]
        lo = jnp.maximum(pos + start, 0)
        hi = jnp.minimum(pos + start + win, seq_len)
        cnt = (hi - lo).astype(F32)
        o_ref[:, cols] = (acc / cnt - ext_ref[HALO:HALO + tl, cols]).astype(o_ref.dtype)


def _pool(z, width, seq_len):
    m = z.shape[0]
    tl = _pick(seq_len, 256)
    prev, nxt = _halo_specs(tl, width, 5, m)
    return pl.pallas_call(
        functools.partial(_pool_body, tl=tl, tps=seq_len // tl, seq_len=seq_len,
                          group=width // len(POOL_WINDOWS)),
        grid=(m // tl,),
        in_specs=[pl.BlockSpec((tl, width), lambda i: (i, 5)), prev, nxt],
        out_specs=pl.BlockSpec((tl, width), lambda i: (i, 0)),
        out_shape=jax.ShapeDtypeStruct((m, width), BF16),
        scratch_shapes=[pltpu.VMEM((tl + 2 * HALO, width), F32)],
        compiler_params=_cparams("arbitrary"),
        name="pool",
    )(z, z, z)


def _conv_d_body(main_ref, prev_ref, next_ref, w_ref, cb_ref, lg_ref, lb_ref, o_ref, ext_ref, sh_ref,
                 *, tl, tps, width):
    _, first, last = _seq_edges(tps)

    def glu(blk):
        blk = blk.astype(F32)
        return blk[:, :width] * jax.nn.sigmoid(blk[:, width:])

    _fill_ext(ext_ref, glu(prev_ref[...]), glu(main_ref[...]), glu(next_ref[...]), first, last, tl)
    rows = sh_ref.shape[1]
    for r in range(SUBLANES - 1):
        sh_ref[r] = ext_ref[r + 1:r + 1 + rows, :]
    taps = w_ref.shape[0]
    pad = taps // 2
    acc = None
    for k in range(taps):
        off = HALO - pad + k
        if off % SUBLANES == 0:
            tap = ext_ref[off:off + tl, :]
        else:
            base = off - off % SUBLANES
            tap = sh_ref[off % SUBLANES - 1, base:base + tl, :]
        term = w_ref[k:k + 1, :] * tap
        acc = term if acc is None else acc + term
    y = acc + cb_ref[...]
    mu = jnp.mean(y, axis=-1, keepdims=True)
    yc = y - mu
    var = jnp.mean(yc * yc, axis=-1, keepdims=True)
    yn = yc * lax.rsqrt(var + LN_EPS) * lg_ref[...] + lb_ref[...]
    o_ref[...] = (yn * jax.nn.sigmoid(yn)).astype(o_ref.dtype)


def _conv_d(z, conv_w, conv_b, ln_g, ln_b, width, seq_len):
    m = z.shape[0]
    tl = _pick(seq_len, 256)
    prev, nxt = _halo_specs(tl, 2 * width, 0, m)
    vec = pl.BlockSpec((1, width), lambda i: (0, 0))
    return pl.pallas_call(
        functools.partial(_conv_d_body, tl=tl, tps=seq_len // tl, width=width),
        grid=(m // tl,),
        in_specs=[
            pl.BlockSpec((tl, 2 * width), lambda i: (i, 0)),
            prev, nxt,
            pl.BlockSpec(conv_w.shape, lambda i: (0, 0)),
            vec, vec, vec,
        ],
        out_specs=pl.BlockSpec((tl, width), lambda i: (i, 0)),
        out_shape=jax.ShapeDtypeStruct((m, width), BF16),
        scratch_shapes=[pltpu.VMEM((tl + 2 * HALO, width), F32),
                        pltpu.VMEM((SUBLANES - 1, tl + 2 * HALO - SUBLANES, width), F32)],
        compiler_params=_cparams("arbitrary"),
        name="conv_d",
    )(z, z, z, conv_w, conv_b.reshape(1, width), ln_g.reshape(1, width), ln_b.reshape(1, width))


def _ffn_down_body(gate_ref, val_ref, prev_ref, next_ref, cw_ref, cb_ref, w_ref, r_ref, g_ref, o_ref,
                   ext_ref, acc_ref, *, tl, tps):
    k = pl.program_id(1)
    _, first, last = _seq_edges(tps)
    _fill_ext(ext_ref, prev_ref[...].astype(F32), gate_ref[...].astype(F32), next_ref[...].astype(F32),
              first, last, tl)
    taps = cw_ref.shape[0]
    pad = taps // 2
    conv = cw_ref[0:1, :] * ext_ref[HALO - pad:HALO - pad + tl, :]
    for t in range(1, taps):
        conv = conv + cw_ref[t:t + 1, :] * ext_ref[HALO - pad + t:HALO - pad + t + tl, :]
    g = conv + cb_ref[...]
    f = (g * jax.nn.sigmoid(g) * val_ref[...].astype(F32)).astype(w_ref.dtype)
    part = jnp.dot(f, w_ref[...], preferred_element_type=F32)

    @pl.when(k == 0)
    def _():
        acc_ref[...] = part

    @pl.when(k > 0)
    def _():
        acc_ref[...] += part

    @pl.when(k == pl.num_programs(1) - 1)
    def _():
        o_ref[...] = r_ref[...] + g_ref[...] * acc_ref[...]


def _ffn_down(u, conv_w, conv_b, w_down, res, mods, k_gate, seq_len, rows_per_mod):
    m = u.shape[0]
    d_ff, d = w_down.shape
    tl = _pick(seq_len, 512)
    tk = max(t for t in range(LANE, min(d_ff, 1536) + 1, LANE) if d_ff % t == 0)
    nk = d_ff // tk
    per = tl // HALO
    last_blk = m // HALO - 1
    return pl.pallas_call(
        functools.partial(_ffn_down_body, tl=tl, tps=seq_len // tl),
        grid=(m // tl, nk),
        in_specs=[
            pl.BlockSpec((tl, tk), lambda i, k: (i, k)),
            pl.BlockSpec((tl, tk), lambda i, k: (i, nk + k)),
            pl.BlockSpec((HALO, tk), lambda i, k: (jnp.maximum(i * per - 1, 0), k)),
            pl.BlockSpec((HALO, tk), lambda i, k: (jnp.minimum((i + 1) * per, last_blk), k)),
            pl.BlockSpec((conv_w.shape[0], tk), lambda i, k: (0, k)),
            pl.BlockSpec((1, tk), lambda i, k: (0, k)),
            pl.BlockSpec((tk, d), lambda i, k: (k, 0)),
            pl.BlockSpec((tl, d), lambda i, k: (i, 0)),
            pl.BlockSpec((None, 1, d), lambda i, k: (i * tl // rows_per_mod, 0, k_gate)),
        ],
        out_specs=pl.BlockSpec((tl, d), lambda i, k: (i, 0)),
        out_shape=jax.ShapeDtypeStruct((m, d), F32),
        scratch_shapes=[pltpu.VMEM((tl + 2 * HALO, tk), F32), pltpu.VMEM((tl, d), F32)],
        compiler_params=_cparams("arbitrary", "arbitrary"),
        name="ffn_down",
    )(u, u, u, u, conv_w, conv_b.reshape(1, d_ff), w_down, res, mods)


def _rope_vreg(bn, cos_ref, sin_ref):
    return bn * cos_ref[...] + pltpu.roll(bn, LANE // 2, 1) * sin_ref[...]


def _qkv_prep_body(z_ref, qg_ref, wq_ref, kvg_ref, wkv_ref, qhg_ref, khg_ref, cos_ref, sin_ref,
                   q_ref, k_ref, v_ref, *, q_lora, kv_lora, heads, qk_dim, scale):
    z = z_ref[...].astype(F32)
    lane = lax.broadcasted_iota(jnp.int32, (1, LANE), 1)
    rope_lanes = lane < (LANE // 2)
    slot = 2 * LANE

    def rms(v, g):
        return v * lax.rsqrt(jnp.mean(v * v, axis=-1, keepdims=True) + EPS) * g

    qn = rms(z[:, :q_lora], qg_ref[...]).astype(BF16)
    yq = jnp.dot(qn, wq_ref[...], preferred_element_type=F32)
    g_nope, g_rope = qhg_ref[:, :LANE], qhg_ref[:, LANE:]
    for h in range(heads):
        a = yq[:, h * slot:h * slot + LANE]
        b = yq[:, h * slot + LANE:(h + 1) * slot]
        ss = jnp.sum(a * a + jnp.where(rope_lanes, b * b, 0.0), axis=-1, keepdims=True)
        r = lax.rsqrt(ss / qk_dim + EPS) * scale
        q_ref[:, h * slot:h * slot + LANE] = (a * r * g_nope).astype(q_ref.dtype)
        q_ref[:, h * slot + LANE:(h + 1) * slot] = _rope_vreg(b * r * g_rope, cos_ref, sin_ref).astype(q_ref.dtype)

    kvn = rms(z[:, q_lora:q_lora + kv_lora], kvg_ref[...]).astype(BF16)
    ykv = jnp.dot(kvn, wkv_ref[...], preferred_element_type=F32)
    kr = z[:, q_lora + kv_lora:q_lora + kv_lora + LANE]
    kr_ss = jnp.sum(jnp.where(rope_lanes, kr * kr, 0.0), axis=-1, keepdims=True)
    g_nope, g_rope = khg_ref[:, :LANE], khg_ref[:, LANE:]
    for h in range(heads):
        a = ykv[:, h * LANE:(h + 1) * LANE]
        ss = jnp.sum(a * a, axis=-1, keepdims=True) + kr_ss
        r = lax.rsqrt(ss / qk_dim + EPS)
        k_ref[:, h * slot:h * slot + LANE] = (a * r * g_nope).astype(k_ref.dtype)
        k_ref[:, h * slot + LANE:(h + 1) * slot] = _rope_vreg(kr * r * g_rope, cos_ref, sin_ref).astype(k_ref.dtype)
    ones = jnp.ones((z.shape[0], LANE), v_ref.dtype)
    for h in range(heads):
        v_ref[:, h * slot:h * slot + LANE] = ykv[:, (heads + h) * LANE:(heads + h + 1) * LANE].astype(v_ref.dtype)
        v_ref[:, h * slot + LANE:(h + 1) * slot] = ones


def _qkv_prep(z, lw, cos_t, sin_t, seq_len, dims, use_rope):
    m = z.shape[0]
    heads, q_lora, kv_lora, qk_dim = dims["heads"], dims["q_lora"], dims["kv_lora"], dims["qk_dim"]
    nq = dims["nq"]
    tm = _pick(seq_len, 512)
    tps = seq_len // tm
    col_block = (z.shape[1] - nq) // nq
    if use_rope:
        tab = pl.BlockSpec((tm, LANE), lambda i: (i % tps, 0))
    else:
        tab = pl.BlockSpec((tm, LANE), lambda i: (0, 0))

    def full(a):
        return pl.BlockSpec(a.shape, lambda i: (0, 0))

    qw = heads * 2 * LANE
    return pl.pallas_call(
        functools.partial(_qkv_prep_body, q_lora=q_lora, kv_lora=kv_lora, heads=heads, qk_dim=qk_dim,
                          scale=qk_dim ** -0.5 * LOG2E),
        grid=(m // tm,),
        in_specs=[
            pl.BlockSpec((tm, nq), lambda i: (i, col_block)),
            full(lw["q_norm_g"]), full(lw["wq"]), full(lw["kv_norm_g"]), full(lw["wkv"]),
            full(lw["q_head_g"]), full(lw["k_head_g"]), tab, tab,
        ],
        out_specs=[
            pl.BlockSpec((tm, qw), lambda i: (i, 0)),
            pl.BlockSpec((tm, qw), lambda i: (i, 0)),
            pl.BlockSpec((tm, qw), lambda i: (i, 0)),
        ],
        out_shape=[
            jax.ShapeDtypeStruct((m, qw), BF16),
            jax.ShapeDtypeStruct((m, qw), BF16),
            jax.ShapeDtypeStruct((m, qw), BF16),
        ],
        compiler_params=_cparams("arbitrary"),
        name="qkv_prep",
    )(z, lw["q_norm_g"], lw["wq"], lw["kv_norm_g"], lw["wkv"], lw["q_head_g"], lw["k_head_g"], cos_t, sin_t)


def _attn_update(q, k, v1, carry):
    m_i, acc = carry
    s = lax.dot_general(q, k, (((1,), (1,)), ((), ())), preferred_element_type=F32)
    m_new = jnp.maximum(m_i, jnp.max(s, axis=-1, keepdims=True))
    alpha = jnp.exp2(m_i - m_new)
    p = jnp.exp2(s - m_new)
    acc = alpha * acc + jnp.dot(p.astype(v1.dtype), v1, preferred_element_type=F32)
    return m_new, acc


def _attn_body(*refs, tk, n_lat):
    if n_lat:
        q_ref, kc_ref, vc_ref, kx_ref, vx_ref, o_ref = refs
    else:
        q_ref, kc_ref, vc_ref, o_ref = refs
    q = q_ref[...]
    tq = q.shape[0]
    vd = o_ref.shape[-1]
    carry = (jnp.full((tq, 1), -jnp.inf, F32), jnp.zeros((tq, 2 * vd), F32))
    carry = _attn_update(q, kc_ref[...], vc_ref[...], carry)
    for c in range(n_lat):
        carry = _attn_update(q, kx_ref[c * tk:(c + 1) * tk, :], vx_ref[c * tk:(c + 1) * tk, :], carry)
    _, acc = carry
    o_ref[...] = (acc[:, :vd] / acc[:, vd:]).astype(o_ref.dtype)


def _attention(q, kc, vc, kx, vx, batch, heads):
    lq = q.shape[0] // batch
    lc = kc.shape[0] // batch
    slot, vd = 2 * LANE, LANE
    tq = _pick(lq, 512)
    q3 = q.reshape(batch, lq, heads * slot)
    args = [q3, kc.reshape(batch, lc, heads * slot), vc.reshape(batch, lc, heads * slot)]
    in_specs = [
        pl.BlockSpec((None, tq, slot), lambda b, h, i: (b, i, h)),
        pl.BlockSpec((None, lc, slot), lambda b, h, i: (b, 0, h)),
        pl.BlockSpec((None, lc, slot), lambda b, h, i: (b, 0, h)),
    ]
    n_lat, tk = 0, 0
    if kx is not None:
        lx = kx.shape[0] // batch
        tk = _pick(lx, 512)
        n_lat = lx // tk
        args += [kx.reshape(batch, lx, heads * slot), vx.reshape(batch, lx, heads * slot)]
        in_specs += [
            pl.BlockSpec((None, lx, slot), lambda b, h, i: (b, 0, h)),
            pl.BlockSpec((None, lx, slot), lambda b, h, i: (b, 0, h)),
        ]
    out = pl.pallas_call(
        functools.partial(_attn_body, tk=tk, n_lat=n_lat),
        grid=(batch, heads, lq // tq),
        in_specs=in_specs,
        out_specs=pl.BlockSpec((None, tq, vd), lambda b, h, i: (b, i, h)),
        out_shape=jax.ShapeDtypeStruct((batch, lq, heads * vd), BF16),
        compiler_params=_cparams("arbitrary", "arbitrary", "arbitrary"),
        name="attention_x" if n_lat else "attention_c",
    )(*args)
    return out.reshape(batch * lq, heads * vd)


def _merge_body(h_ref, a_ref, t_ref, p_ref, d_ref, wg_ref, wa_ref, wt_ref, wp_ref, ps_ref, wd_ref, o_ref):
    h = h_ref[...]

    def gate(i):
        return jax.nn.sigmoid(jnp.dot(h, wg_ref[i], preferred_element_type=F32))

    def proj(x_ref, w):
        return jnp.dot(x_ref[...], w, preferred_element_type=F32)

    acc = gate(0) * proj(a_ref, wa_ref[...])
    acc = acc + gate(1) * proj(t_ref, wt_ref[...])
    acc = acc + gate(2) * (proj(p_ref, wp_ref[...]) * ps_ref[...])
    acc = acc + gate(3) * proj(d_ref, wd_ref[...])
    o_ref[...] = acc.astype(o_ref.dtype)


def _merge(h, act_a, att, act_p, act_d, lw, rows_per_seq):
    m, d = h.shape
    n_branch = lw["wg"].shape[0]
    tn = d // n_branch
    group = act_p.shape[1] // n_branch
    tm = _pick(rows_per_seq, 512)

    def act(a):
        return pl.BlockSpec((tm, a.shape[1]), lambda j, i: (i, 0))

    def wcol(w):
        return pl.BlockSpec((w.shape[0], tn), lambda j, i: (0, j))

    return pl.pallas_call(
        _merge_body,
        grid=(n_branch, m // tm),
        in_specs=[
            act(h), act(act_a), act(att),
            pl.BlockSpec((tm, group), lambda j, i: (i, j)),
            act(act_d),
            pl.BlockSpec((n_branch, d, tn), lambda j, i: (0, 0, j)),
            wcol(lw["w_a_out"]), wcol(lw["w_mla_out"]),
            pl.BlockSpec((None, group, tn), lambda j, i: (j, 0, 0)),
            pl.BlockSpec((1, tn), lambda j, i: (0, j)),
            wcol(lw["w_d_out"]),
        ],
        out_specs=pl.BlockSpec((tm, tn), lambda j, i: (i, j)),
        out_shape=jax.ShapeDtypeStruct((m, d), BF16),
        compiler_params=_cparams("arbitrary", "arbitrary"),
        name="merge",
    )(h, act_a, att, act_p, act_d, lw["wg"], lw["w_a_out"], lw["w_mla_out"], lw["w_pool"],
      lw["pool_scale"], lw["w_d_out"])


def _rope_perm(rope_dim):
    nf = rope_dim // 4
    j = jnp.arange(rope_dim)
    return jnp.where((j % (2 * nf)) < nf, j + nf, j - nf)


def _rope_tables(seq_len, rope_dim):
    rows = seq_len // GRID_W
    row = jnp.broadcast_to(jnp.arange(rows)[:, None], (rows, GRID_W)).reshape(seq_len)
    col = jnp.broadcast_to(jnp.arange(GRID_W)[None, :], (rows, GRID_W)).reshape(seq_len)
    nf = rope_dim // 4
    inv = ROPE_THETA ** (-jnp.arange(nf, dtype=F32) / nf)
    ang = jnp.stack([row, col], axis=-1).astype(F32)[:, :, None] * inv
    cos, sin = jnp.cos(ang), jnp.sin(ang)
    cos_t = jnp.concatenate([cos, cos], axis=-1).reshape(seq_len, rope_dim)
    sin_t = jnp.concatenate([-sin, sin], axis=-1).reshape(seq_len, rope_dim)
    pad = jnp.zeros((seq_len, LANE - rope_dim), F32)
    return jnp.concatenate([cos_t, pad], axis=-1), jnp.concatenate([sin_t, pad], axis=-1)


def _layer_weights(l, p, dims):
    d, wa, wc, wd = dims["d"], dims["wa"], dims["wc"], dims["wd"]
    heads, nope, rope, vdim = dims["heads"], dims["nope"], dims["rope"], dims["vdim"]
    q_lora, kv_lora, nq = dims["q_lora"], dims["kv_lora"], dims["nq"]
    perm = _rope_perm(rope)
    w_in = p["w_in"][l]
    off_q = 3 * wa
    off_kv = off_q + q_lora
    off_kr = off_kv + kv_lora
    off_p = off_kr + rope
    off_d = off_p + wc
    off_g = off_d + 2 * wd
    kr = w_in[:, off_kr:off_p]
    qkv_used = q_lora + kv_lora + 2 * rope
    w_z = jnp.concatenate([
        w_in[:, off_d:off_g],
        w_in[:, wa:3 * wa],
        w_in[:, :wa],
        w_in[:, off_p:off_d],
        w_in[:, off_q:off_kr],
        kr, kr[:, perm],
        jnp.zeros((d, nq - qkv_used), w_in.dtype),
    ], axis=1).astype(BF16)
    n_branch = (w_in.shape[1] - off_g) // d
    wg = w_in[:, off_g:].reshape(d, n_branch, d).transpose(1, 0, 2).astype(BF16)

    qk = nope + rope
    wq = p["w_q_up"][l].reshape(q_lora, heads, qk)
    wq = jnp.concatenate([wq, wq[:, :, nope:][:, :, perm]], axis=-1).reshape(q_lora, heads * 2 * LANE)
    wkv = p["w_kv_up"][l].reshape(kv_lora, heads, nope + vdim)
    wkv = jnp.concatenate([wkv[:, :, :nope].reshape(kv_lora, heads * nope),
                           wkv[:, :, nope:].reshape(kv_lora, heads * vdim)], axis=1)

    def head_gain(g):
        return jnp.concatenate([g, g[nope:][perm]]).reshape(1, 2 * LANE)

    return {
        "w_z": w_z, "wg": wg,
        "wq": wq.astype(BF16), "wkv": wkv.astype(BF16),
        "q_norm_g": p["q_norm_g"][l].reshape(1, q_lora), "kv_norm_g": p["kv_norm_g"][l].reshape(1, kv_lora),
        "q_head_g": head_gain(p["q_head_g"][l]), "k_head_g": head_gain(p["k_head_g"][l]),
        "w_a_out": p["w_a_out"][l].astype(BF16), "w_mla_out": p["w_mla_out"][l].astype(BF16),
        "w_pool": p["w_pool"][l].astype(BF16), "pool_scale": p["pool_scale"][l].reshape(1, d),
        "w_d_out": p["w_d_out"][l].astype(BF16), "w_out": p["w_out"][l].astype(BF16),
        "w_up": p["w_up"][l].astype(BF16), "w_down": p["w_down"][l].astype(BF16),
    }


def _token_mixer_inputs(xs, mods, rows_per_mod, seq_len, l, p, lw, dims, tables):
    z, h = _norm_mm(xs, p["norm1_g"][l], mods, 0, rows_per_mod, lw["w_z"], tn=1536, emit_h=True)
    q, k, v = _qkv_prep(z, lw, tables[0], tables[1], seq_len, dims, tables[2])
    return z, h, q, k, v


def _finish_layer(xs, mods, rows_per_mod, seq_len, l, p, lw, dims, z, h, att):
    wa, wc, wd = dims["wa"], dims["wc"], dims["wd"]
    act_a = _conv_a(z, p["conv_a_w"][l], wa, seq_len)
    act_p = _pool(z, wc, seq_len)
    act_d = _conv_d(z, p["conv_d_w"][l], p["conv_d_b"][l], p["cd_ln_g"][l], p["cd_ln_b"][l], wd, seq_len)
    merged = _merge(h, act_a, att, act_p, act_d, lw, seq_len)
    x1 = _mm_res(merged, lw["w_out"], xs, mods, 2, rows_per_mod, tm=1024, tn=1024)
    u = _norm_mm(x1, p["norm2_g"][l], mods, 3, rows_per_mod, lw["w_up"], tn=1024, emit_h=False)
    return _ffn_down(u, p["conv_ff_w"][l], p["conv_ff_b"][l], lw["w_down"], x1, mods, 5, seq_len, rows_per_mod)


def kernel(x, c, ctx, c_ctx, ada_w, ada_b, norm1_g, w_in, conv_a_w, w_a_out, q_norm_g, w_q_up, kv_norm_g, w_kv_up, q_head_g, k_head_g, w_mla_out, w_pool, pool_scale, conv_d_w, conv_d_b, cd_ln_g, cd_ln_b, w_d_out, w_out, norm2_g, w_up, conv_ff_w, conv_ff_b, w_down):
    p = dict(norm1_g=norm1_g, w_in=w_in, conv_a_w=conv_a_w, w_a_out=w_a_out, q_norm_g=q_norm_g, w_q_up=w_q_up,
             kv_norm_g=kv_norm_g, w_kv_up=w_kv_up, q_head_g=q_head_g, k_head_g=k_head_g, w_mla_out=w_mla_out,
             w_pool=w_pool, pool_scale=pool_scale, conv_d_w=conv_d_w, conv_d_b=conv_d_b, cd_ln_g=cd_ln_g,
             cd_ln_b=cd_ln_b, w_d_out=w_d_out, w_out=w_out, norm2_g=norm2_g, w_up=w_up, conv_ff_w=conv_ff_w,
             conv_ff_b=conv_ff_b, w_down=w_down)
    batch, seq, d = x.shape
    n_ctx = ctx.shape[1]
    depth = ada_w.shape[0]
    qk_dim = q_head_g.shape[1]
    heads = w_q_up.shape[2] // qk_dim
    vdim = w_mla_out.shape[1] // heads
    nope = w_kv_up.shape[2] // heads - vdim
    rope = qk_dim - nope
    wa, wc, wd = conv_a_w.shape[2], w_pool.shape[1] * w_pool.shape[2], conv_d_w.shape[2]
    q_lora, kv_lora = w_q_up.shape[1], w_kv_up.shape[1]
    assert nope == LANE and vdim == LANE and 2 * rope == LANE, "head layout assumes 128 | 64 | 128 dims"
    assert wa == wc == wd and wa % (len(POOL_WINDOWS) * LANE) == 0
    nq = q_lora + kv_lora + 2 * rope
    while (6 * wa) % nq or nq % LANE:
        nq += LANE
    dims = dict(d=d, wa=wa, wc=wc, wd=wd, heads=heads, nope=nope, rope=rope, vdim=vdim, qk_dim=qk_dim,
                q_lora=q_lora, kv_lora=kv_lora, nq=nq)

    rows = -(-(batch + 1) // 8) * 8
    cc = jnp.zeros((rows, d), F32).at[:batch].set(c).at[batch].set(c_ctx)
    mods = _ada(cc, ada_w, ada_b)

    cos_t, sin_t = _rope_tables(seq, rope)
    ones_t = jnp.concatenate([jnp.ones((n_ctx, rope), F32), jnp.zeros((n_ctx, LANE - rope), F32)], axis=-1)
    zeros_t = jnp.zeros((n_ctx, LANE), F32)
    tab_x = (cos_t, sin_t, True)
    tab_c = (ones_t, zeros_t, False)

    xs = x.reshape(batch * seq, d)
    cs = ctx.reshape(batch * n_ctx, d)
    for l in range(depth):
        last = l == depth - 1
        lw = _layer_weights(l, p, dims)
        mods_x = mods[l, :batch].reshape(batch, 1, -1)
        mods_c = mods[l, batch:batch + 1].reshape(1, 1, -1)
        zc, hc, qc, kc, vc = _token_mixer_inputs(cs, mods_c, batch * n_ctx, n_ctx, l, p, lw, dims, tab_c)
        zx, hx, qx, kx, vx = _token_mixer_inputs(xs, mods_x, seq, seq, l, p, lw, dims, tab_x)
        att_x = _attention(qx, kc, vc, kx, vx, batch, heads)
        xs_new = _finish_layer(xs, mods_x, seq, seq, l, p, lw, dims, zx, hx, att_x)
        if not last:
            att_c = _attention(qc, kc, vc, None, None, batch, heads)
            cs = _finish_layer(cs, mods_c, batch * n_ctx, n_ctx, l, p, lw, dims, zc, hc, att_c)
        xs = xs_new
    return xs.reshape(batch, seq, d)
```

```python
import functools

import jax
import jax.numpy as jnp
from jax import lax
from jax.experimental import pallas as pl
from jax.experimental.pallas import tpu as pltpu

GRID_W = 64
ROPE_THETA = 10000.0
POOL_WINDOWS = (2, 4, 8, 16)
EPS = 1e-6
LN_EPS = 1e-5

LOG2E = 1.4426950408889634
LANE = 128
SUBLANES = 8
HALO = 16
VMEM_LIMIT = 56 * 1024 * 1024

F32 = jnp.float32
BF16 = jnp.bfloat16


def _cparams(*sem):
    return pltpu.CompilerParams(dimension_semantics=sem, vmem_limit_bytes=VMEM_LIMIT)


def _pick(n, pref):
    if n <= pref:
        return n
    t = pref
    while n % t:
        t //= 2
    return t


def _ada_body(cc_ref, w_ref, b_ref, o_ref):
    cc = cc_ref[...]
    s = (cc * jax.nn.sigmoid(cc)).astype(BF16)
    o_ref[...] = jnp.dot(s, w_ref[...].astype(BF16), preferred_element_type=F32) + b_ref[...]


def _ada(cc, ada_w, ada_b):
    depth, d, n = ada_w.shape
    rows = cc.shape[0]
    tn = _pick(n, 1024)
    return pl.pallas_call(
        _ada_body,
        grid=(depth, n // tn),
        in_specs=[
            pl.BlockSpec((rows, d), lambda l, j: (0, 0)),
            pl.BlockSpec((None, d, tn), lambda l, j: (l, 0, j)),
            pl.BlockSpec((None, 1, tn), lambda l, j: (l, 0, j)),
        ],
        out_specs=pl.BlockSpec((None, rows, tn), lambda l, j: (l, 0, j)),
        out_shape=jax.ShapeDtypeStruct((depth, rows, n), F32),
        compiler_params=_cparams("arbitrary", "arbitrary"),
        name="ada",
    )(cc, ada_w, ada_b.reshape(depth, 1, n))


def _norm_mm_body(x_ref, g_ref, sh_ref, sc_ref, w_ref, z_ref, h_ref):
    @pl.when(pl.program_id(1) == 0)
    def _():
        x = x_ref[...]
        ms = jnp.mean(x * x, axis=-1, keepdims=True)
        y = x * lax.rsqrt(ms + EPS) * g_ref[...]
        h_ref[...] = (y * (1.0 + sc_ref[...]) + sh_ref[...]).astype(h_ref.dtype)

    z_ref[...] = jnp.dot(h_ref[...], w_ref[...], preferred_element_type=F32).astype(z_ref.dtype)


def _norm_mm(x, gain, mods, k_shift, rows_per_mod, w, *, tn):
    m, d = x.shape
    n = w.shape[1]
    tm = _pick(rows_per_mod, 1024)
    tn = _pick(n, tn)
    in_specs = [
        pl.BlockSpec((tm, d), lambda i, j: (i, 0)),
        pl.BlockSpec((1, d), lambda i, j: (0, 0)),
        pl.BlockSpec((None, 1, d), lambda i, j: (i * tm // rows_per_mod, 0, k_shift)),
        pl.BlockSpec((None, 1, d), lambda i, j: (i * tm // rows_per_mod, 0, k_shift + 1)),
        pl.BlockSpec((d, tn), lambda i, j: (0, j)),
    ]
    return pl.pallas_call(
        _norm_mm_body,
        grid=(m // tm, n // tn),
        in_specs=in_specs,
        out_specs=[pl.BlockSpec((tm, tn), lambda i, j: (i, j)), pl.BlockSpec((tm, d), lambda i, j: (i, 0))],
        out_shape=[jax.ShapeDtypeStruct((m, n), BF16), jax.ShapeDtypeStruct((m, d), BF16)],
        compiler_params=_cparams("arbitrary", "arbitrary"),
        name="norm_mm",
    )(x, gain.reshape(1, d), mods, mods, w)


def _mm_res_body(a_ref, w_ref, r_ref, gate_ref, o_ref):
    acc = jnp.dot(a_ref[...], w_ref[...], preferred_element_type=F32)
    o_ref[...] = r_ref[...] + gate_ref[...] * acc


def _mm_res(a, w, res, mods, k_gate, rows_per_mod, *, tm, tn):
    m, k = a.shape
    n = w.shape[1]
    tm = _pick(rows_per_mod, tm)
    tn = _pick(n, tn)
    nj = n // tn
    return pl.pallas_call(
        _mm_res_body,
        grid=(m // tm, nj),
        in_specs=[
            pl.BlockSpec((tm, k), lambda i, j: (i, 0)),
            pl.BlockSpec((k, tn), lambda i, j: (0, j)),
            pl.BlockSpec((tm, tn), lambda i, j: (i, j)),
            pl.BlockSpec((None, 1, tn), lambda i, j: (i * tm // rows_per_mod, 0, k_gate * nj + j)),
        ],
        out_specs=pl.BlockSpec((tm, tn), lambda i, j: (i, j)),
        out_shape=jax.ShapeDtypeStruct((m, n), F32),
        compiler_params=_cparams("arbitrary", "arbitrary"),
        name="mm_res",
    )(a, w, res, mods)


def _halo_specs(tl, width, col_block, n_rows):
    per = tl // HALO
    last = n_rows // HALO - 1
    prev = pl.BlockSpec((HALO, width), lambda i, *_: (jnp.maximum(i * per - 1, 0), col_block))
    nxt = pl.BlockSpec((HALO, width), lambda i, *_: (jnp.minimum((i + 1) * per, last), col_block))
    return prev, nxt


def _fill_ext(ext_ref, prev, main, nxt, first, last, tl):
    ext_ref[0:HALO, :] = jnp.where(first, 0.0, prev)
    ext_ref[HALO:HALO + tl, :] = main
    ext_ref[HALO + tl:HALO + tl + HALO, :] = jnp.where(last, 0.0, nxt)


def _seq_edges(tps):
    s = pl.program_id(0) % tps
    return s, s == 0, s == tps - 1


def _conv_a_body(main_ref, b_ref, prev_ref, next_ref, w_ref, o_ref, ext_ref, *, tl, tps, width):
    _, first, last = _seq_edges(tps)

    def prod(blk):
        blk = blk.astype(F32)
        return blk[:, :width] * blk[:, width:]

    _fill_ext(ext_ref, prod(prev_ref[...]), prod(main_ref[...]), prod(next_ref[...]), first, last, tl)
    taps = w_ref.shape[0]
    pad = taps // 2
    acc = w_ref[0:1, :] * ext_ref[HALO - pad:HALO - pad + tl, :]
    for k in range(1, taps):
        acc = acc + w_ref[k:k + 1, :] * ext_ref[HALO - pad + k:HALO - pad + k + tl, :]
    o_ref[...] = (b_ref[...].astype(F32) * acc).astype(o_ref.dtype)


def _conv_a(z, conv_w, width, seq_len):
    m = z.shape[0]
    tl = _pick(seq_len, 256)
    prev, nxt = _halo_specs(tl, 2 * width, 1, m)
    return pl.pallas_call(
        functools.partial(_conv_a_body, tl=tl, tps=seq_len // tl, width=width),
        grid=(m // tl,),
        in_specs=[
            pl.BlockSpec((tl, 2 * width), lambda i: (i, 1)),
            pl.BlockSpec((tl, width), lambda i: (i, 4)),
            prev, nxt,
            pl.BlockSpec(conv_w.shape, lambda i: (0, 0)),
        ],
        out_specs=pl.BlockSpec((tl, width), lambda i: (i, 0)),
        out_shape=jax.ShapeDtypeStruct((m, width), BF16),
        scratch_shapes=[pltpu.VMEM((tl + 2 * HALO, width), F32)],
        compiler_params=_cparams("arbitrary"),
        name="conv_a",
    )(z, z, z, z, conv_w)


def _pool_body(main_ref, prev_ref, next_ref, o_ref, ext_ref, *, tl, tps, seq_len, group):
    s, first, last = _seq_edges(tps)
    _fill_ext(ext_ref, prev_ref[...].astype(F32), main_ref[...].astype(F32), next_ref[...].astype(F32),
              first, last, tl)
    pos = s * tl + lax.broadcasted_iota(jnp.int32, (tl, 1), 0)
    for g, win in enumerate(POOL_WINDOWS):
        cols = slice(g * group, (g + 1) * group)
        start = -(win // 2)
        acc = ext_ref[HALO + start:HALO + start + tl, cols]
        for j in range(start + 1, start + win):
            acc = acc + ext_ref[HALO + j:HALO + j + tl, cols]
        lo = jnp.maximum(pos + start, 0)
        hi = jnp.minimum(pos + start + win, seq_len)
        cnt = (hi - lo).astype(F32)
        o_ref[:, cols] = (acc / cnt - ext_ref[HALO:HALO + tl, cols]).astype(o_ref.dtype)


def _pool(z, width, seq_len):
    m = z.shape[0]
    tl = _pick(seq_len, 256)
    prev, nxt = _halo_specs(tl, width, 5, m)
    return pl.pallas_call(
        functools.partial(_pool_body, tl=tl, tps=seq_len // tl, seq_len=seq_len,
                          group=width // len(POOL_WINDOWS)),
        grid=(m // tl,),
        in_specs=[pl.BlockSpec((tl, width), lambda i: (i, 5)), prev, nxt],
        out_specs=pl.BlockSpec((tl, width), lambda i: (i, 0)),
        out_shape=jax.ShapeDtypeStruct((m, width), BF16),
        scratch_shapes=[pltpu.VMEM((tl + 2 * HALO, width), F32)],
        compiler_params=_cparams("arbitrary"),
        name="pool",
    )(z, z, z)


def _conv_d_body(main_ref, prev_ref, next_ref, w_ref, cb_ref, lg_ref, lb_ref, o_ref, ext_ref, sh_ref,
                 *, tl, tps, width):
    _, first, last = _seq_edges(tps)

    def glu(blk):
        blk = blk.astype(F32)
        return blk[:, :width] * jax.nn.sigmoid(blk[:, width:])

    _fill_ext(ext_ref, glu(prev_ref[...]), glu(main_ref[...]), glu(next_ref[...]), first, last, tl)
    rows = sh_ref.shape[1]
    for r in range(SUBLANES - 1):
        sh_ref[r] = ext_ref[r + 1:r + 1 + rows, :]
    taps = w_ref.shape[0]
    pad = taps // 2
    acc = None
    for k in range(taps):
        off = HALO - pad + k
        if off % SUBLANES == 0:
            tap = ext_ref[off:off + tl, :]
        else:
            base = off - off % SUBLANES
            tap = sh_ref[off % SUBLANES - 1, base:base + tl, :]
        term = w_ref[k:k + 1, :] * tap
        acc = term if acc is None else acc + term
    y = acc + cb_ref[...]
    mu = jnp.mean(y, axis=-1, keepdims=True)
    yc = y - mu
    var = jnp.mean(yc * yc, axis=-1, keepdims=True)
    yn = yc * lax.rsqrt(var + LN_EPS) * lg_ref[...] + lb_ref[...]
    o_ref[...] = (yn * jax.nn.sigmoid(yn)).astype(o_ref.dtype)


def _conv_d(z, conv_w, conv_b, ln_g, ln_b, width, seq_len):
    m = z.shape[0]
    tl = _pick(seq_len, 256)
    prev, nxt = _halo_specs(tl, 2 * width, 0, m)
    vec = pl.BlockSpec((1, width), lambda i: (0, 0))
    return pl.pallas_call(
        functools.partial(_conv_d_body, tl=tl, tps=seq_len // tl, width=width),
        grid=(m // tl,),
        in_specs=[
            pl.BlockSpec((tl, 2 * width), lambda i: (i, 0)),
            prev, nxt,
            pl.BlockSpec(conv_w.shape, lambda i: (0, 0)),
            vec, vec, vec,
        ],
        out_specs=pl.BlockSpec((tl, width), lambda i: (i, 0)),
        out_shape=jax.ShapeDtypeStruct((m, width), BF16),
        scratch_shapes=[pltpu.VMEM((tl + 2 * HALO, width), F32),
                        pltpu.VMEM((SUBLANES - 1, tl + 2 * HALO - SUBLANES, width), F32)],
        compiler_params=_cparams("arbitrary"),
        name="conv_d",
    )(z, z, z, conv_w, conv_b.reshape(1, width), ln_g.reshape(1, width), ln_b.reshape(1, width))


def _ffn_up_body(x_ref, xp_ref, xn_ref, g_ref, sh_ref, sc_ref, w_ref, cw_ref, cb_ref, f_ref, h_ref,
                 *, tm, tps, tc):
    _, first, last = _seq_edges(tps)

    @pl.when(pl.program_id(1) == 0)
    def _():
        def norm(x):
            ms = jnp.mean(x * x, axis=-1, keepdims=True)
            y = x * lax.rsqrt(ms + EPS) * g_ref[...]
            return (y * (1.0 + sc_ref[...]) + sh_ref[...]).astype(h_ref.dtype)

        zero = jnp.zeros((HALO, h_ref.shape[1]), h_ref.dtype)
        h_ref[0:HALO, :] = jnp.where(first, zero, norm(xp_ref[...]))
        h_ref[HALO:HALO + tm, :] = norm(x_ref[...])
        h_ref[HALO + tm:HALO + tm + HALO, :] = jnp.where(last, zero, norm(xn_ref[...]))

    res = jnp.dot(h_ref[...], w_ref[...], preferred_element_type=F32)
    gate = res[:, :tc]
    val = res[HALO:HALO + tm, tc:]
    taps = cw_ref.shape[0]
    pad = taps // 2
    conv = cw_ref[0:1, :] * gate[HALO - pad:HALO - pad + tm, :]
    for t in range(1, taps):
        conv = conv + cw_ref[t:t + 1, :] * gate[HALO - pad + t:HALO - pad + t + tm, :]
    g = conv + cb_ref[...]
    f_ref[...] = (g * jax.nn.sigmoid(g) * val).astype(f_ref.dtype)


def _ffn_up_weight(w_up, tc):
    d, two_ff = w_up.shape
    nj = two_ff // (2 * tc)
    return w_up.reshape(d, 2, nj, tc).transpose(0, 2, 1, 3).reshape(d, two_ff)


def _ffn_chunk(d_ff):
    return _pick(d_ff, 512)


def _ffn_up(x, gain, mods, k_shift, rows_per_mod, seq_len, w_up_grouped, conv_w, conv_b):
    m, d = x.shape
    d_ff = conv_w.shape[1]
    tc = _ffn_chunk(d_ff)
    tm = _pick(seq_len, 1024)
    per = tm // HALO
    last_blk = m // HALO - 1
    return pl.pallas_call(
        functools.partial(_ffn_up_body, tm=tm, tps=seq_len // tm, tc=tc),
        grid=(m // tm, d_ff // tc),
        in_specs=[
            pl.BlockSpec((tm, d), lambda i, j: (i, 0)),
            pl.BlockSpec((HALO, d), lambda i, j: (jnp.maximum(i * per - 1, 0), 0)),
            pl.BlockSpec((HALO, d), lambda i, j: (jnp.minimum((i + 1) * per, last_blk), 0)),
            pl.BlockSpec((1, d), lambda i, j: (0, 0)),
            pl.BlockSpec((None, 1, d), lambda i, j: (i * tm // rows_per_mod, 0, k_shift)),
            pl.BlockSpec((None, 1, d), lambda i, j: (i * tm // rows_per_mod, 0, k_shift + 1)),
            pl.BlockSpec((d, 2 * tc), lambda i, j: (0, j)),
            pl.BlockSpec((conv_w.shape[0], tc), lambda i, j: (0, j)),
            pl.BlockSpec((1, tc), lambda i, j: (0, j)),
        ],
        out_specs=pl.BlockSpec((tm, tc), lambda i, j: (i, j)),
        out_shape=jax.ShapeDtypeStruct((m, d_ff), BF16),
        scratch_shapes=[pltpu.VMEM((tm + 2 * HALO, d), BF16)],
        compiler_params=_cparams("arbitrary", "arbitrary"),
        name="ffn_up",
    )(x, x, x, gain.reshape(1, d), mods, mods, w_up_grouped, conv_w, conv_b.reshape(1, d_ff))


def _rope_vreg(bn, cos_ref, sin_ref):
    return bn * cos_ref[...] + pltpu.roll(bn, LANE // 2, 1) * sin_ref[...]


def _qkv_prep_body(z_ref, qg_ref, wq_ref, kvg_ref, wkv_ref, qhg_ref, khg_ref, cos_ref, sin_ref,
                   q_ref, k_ref, v_ref, *, q_lora, kv_lora, heads, qk_dim, scale):
    z = z_ref[...].astype(F32)
    lane = lax.broadcasted_iota(jnp.int32, (1, LANE), 1)
    rope_lanes = lane < (LANE // 2)
    slot = 2 * LANE

    def rms(v, g):
        return v * lax.rsqrt(jnp.mean(v * v, axis=-1, keepdims=True) + EPS) * g

    qn = rms(z[:, :q_lora], qg_ref[...]).astype(BF16)
    yq = jnp.dot(qn, wq_ref[...], preferred_element_type=F32)
    g_nope, g_rope = qhg_ref[:, :LANE], qhg_ref[:, LANE:]
    for h in range(heads):
        a = yq[:, h * slot:h * slot + LANE]
        b = yq[:, h * slot + LANE:(h + 1) * slot]
        ss = jnp.sum(a * a + jnp.where(rope_lanes, b * b, 0.0), axis=-1, keepdims=True)
        r = lax.rsqrt(ss / qk_dim + EPS) * scale
        q_ref[:, h * slot:h * slot + LANE] = (a * r * g_nope).astype(q_ref.dtype)
        q_ref[:, h * slot + LANE:(h + 1) * slot] = _rope_vreg(b * r * g_rope, cos_ref, sin_ref).astype(q_ref.dtype)

    kvn = rms(z[:, q_lora:q_lora + kv_lora], kvg_ref[...]).astype(BF16)
    ykv = jnp.dot(kvn, wkv_ref[...], preferred_element_type=F32)
    kr = z[:, q_lora + kv_lora:q_lora + kv_lora + LANE]
    kr_ss = jnp.sum(jnp.where(rope_lanes, kr * kr, 0.0), axis=-1, keepdims=True)
    g_nope, g_rope = khg_ref[:, :LANE], khg_ref[:, LANE:]
    for h in range(heads):
        a = ykv[:, h * LANE:(h + 1) * LANE]
        ss = jnp.sum(a * a, axis=-1, keepdims=True) + kr_ss
        r = lax.rsqrt(ss / qk_dim + EPS)
        k_ref[:, h * slot:h * slot + LANE] = (a * r * g_nope).astype(k_ref.dtype)
        k_ref[:, h * slot + LANE:(h + 1) * slot] = _rope_vreg(kr * r * g_rope, cos_ref, sin_ref).astype(k_ref.dtype)
    ones = jnp.ones((z.shape[0], LANE), v_ref.dtype)
    for h in range(heads):
        v_ref[:, h * slot:h * slot + LANE] = ykv[:, (heads + h) * LANE:(heads + h + 1) * LANE].astype(v_ref.dtype)
        v_ref[:, h * slot + LANE:(h + 1) * slot] = ones


def _qkv_prep(z, lw, cos_t, sin_t, seq_len, dims, use_rope):
    m = z.shape[0]
    heads, q_lora, kv_lora, qk_dim = dims["heads"], dims["q_lora"], dims["kv_lora"], dims["qk_dim"]
    nq = dims["nq"]
    tm = _pick(seq_len, 512)
    tps = seq_len // tm
    col_block = (z.shape[1] - nq) // nq
    if use_rope:
        tab = pl.BlockSpec((tm, LANE), lambda i: (i % tps, 0))
    else:
        tab = pl.BlockSpec((tm, LANE), lambda i: (0, 0))

    def full(a):
        return pl.BlockSpec(a.shape, lambda i: (0, 0))

    qw = heads * 2 * LANE
    return pl.pallas_call(
        functools.partial(_qkv_prep_body, q_lora=q_lora, kv_lora=kv_lora, heads=heads, qk_dim=qk_dim,
                          scale=qk_dim ** -0.5 * LOG2E),
        grid=(m // tm,),
        in_specs=[
            pl.BlockSpec((tm, nq), lambda i: (i, col_block)),
            full(lw["q_norm_g"]), full(lw["wq"]), full(lw["kv_norm_g"]), full(lw["wkv"]),
            full(lw["q_head_g"]), full(lw["k_head_g"]), tab, tab,
        ],
        out_specs=[
            pl.BlockSpec((tm, qw), lambda i: (i, 0)),
            pl.BlockSpec((tm, qw), lambda i: (i, 0)),
            pl.BlockSpec((tm, qw), lambda i: (i, 0)),
        ],
        out_shape=[
            jax.ShapeDtypeStruct((m, qw), BF16),
            jax.ShapeDtypeStruct((m, qw), BF16),
            jax.ShapeDtypeStruct((m, qw), BF16),
        ],
        compiler_params=_cparams("arbitrary"),
        name="qkv_prep",
    )(z, lw["q_norm_g"], lw["wq"], lw["kv_norm_g"], lw["wkv"], lw["q_head_g"], lw["k_head_g"], cos_t, sin_t)


def _attn_update(q, k, v1, carry):
    m_i, acc = carry
    s = lax.dot_general(q, k, (((1,), (1,)), ((), ())), preferred_element_type=F32)
    m_new = jnp.maximum(m_i, jnp.max(s, axis=-1, keepdims=True))
    alpha = jnp.exp2(m_i - m_new)
    p = jnp.exp2(s - m_new)
    acc = alpha * acc + jnp.dot(p.astype(v1.dtype), v1, preferred_element_type=F32)
    return m_new, acc


def _attn_body(*refs, tk, n_lat):
    if n_lat:
        q_ref, kc_ref, vc_ref, kx_ref, vx_ref, o_ref = refs
    else:
        q_ref, kc_ref, vc_ref, o_ref = refs
    q = q_ref[...]
    tq = q.shape[0]
    vd = o_ref.shape[-1]
    carry = (jnp.full((tq, 1), -jnp.inf, F32), jnp.zeros((tq, 2 * vd), F32))
    carry = _attn_update(q, kc_ref[...], vc_ref[...], carry)
    for c in range(n_lat):
        carry = _attn_update(q, kx_ref[c * tk:(c + 1) * tk, :], vx_ref[c * tk:(c + 1) * tk, :], carry)
    _, acc = carry
    o_ref[...] = (acc[:, :vd] / acc[:, vd:]).astype(o_ref.dtype)


def _attention(q, kc, vc, kx, vx, batch, heads):
    lq = q.shape[0] // batch
    lc = kc.shape[0] // batch
    slot, vd = 2 * LANE, LANE
    tq = _pick(lq, 1024)
    q3 = q.reshape(batch, lq, heads * slot)
    args = [q3, kc.reshape(batch, lc, heads * slot), vc.reshape(batch, lc, heads * slot)]
    in_specs = [
        pl.BlockSpec((None, tq, slot), lambda b, h, i: (b, i, h)),
        pl.BlockSpec((None, lc, slot), lambda b, h, i: (b, 0, h)),
        pl.BlockSpec((None, lc, slot), lambda b, h, i: (b, 0, h)),
    ]
    n_lat, tk = 0, 0
    if kx is not None:
        lx = kx.shape[0] // batch
        tk = _pick(lx, 1024)
        n_lat = lx // tk
        args += [kx.reshape(batch, lx, heads * slot), vx.reshape(batch, lx, heads * slot)]
        in_specs += [
            pl.BlockSpec((None, lx, slot), lambda b, h, i: (b, 0, h)),
            pl.BlockSpec((None, lx, slot), lambda b, h, i: (b, 0, h)),
        ]
    out = pl.pallas_call(
        functools.partial(_attn_body, tk=tk, n_lat=n_lat),
        grid=(batch, heads, lq // tq),
        in_specs=in_specs,
        out_specs=pl.BlockSpec((None, tq, vd), lambda b, h, i: (b, i, h)),
        out_shape=jax.ShapeDtypeStruct((batch, lq, heads * vd), BF16),
        compiler_params=_cparams("arbitrary", "arbitrary", "arbitrary"),
        name="attention_x" if n_lat else "attention_c",
    )(*args)
    return out.reshape(batch * lq, heads * vd)


def _merge_body(h_ref, a_ref, t_ref, p_ref, d_ref, wg_ref, wa_ref, wt_ref, wp_ref, ps_ref, wd_ref, o_ref):
    h = h_ref[...]

    def gate(i):
        return jax.nn.sigmoid(jnp.dot(h, wg_ref[i], preferred_element_type=F32))

    def proj(x_ref, w):
        return jnp.dot(x_ref[...], w, preferred_element_type=F32)

    acc = gate(0) * proj(a_ref, wa_ref[...])
    acc = acc + gate(1) * proj(t_ref, wt_ref[...])
    acc = acc + gate(2) * (proj(p_ref, wp_ref[...]) * ps_ref[...])
    acc = acc + gate(3) * proj(d_ref, wd_ref[...])
    o_ref[...] = acc.astype(o_ref.dtype)


def _merge(h, act_a, att, act_p, act_d, lw, rows_per_seq):
    m, d = h.shape
    n_branch = lw["wg"].shape[0]
    tn = d // n_branch
    group = act_p.shape[1] // n_branch
    tm = _pick(rows_per_seq, 512)

    def act(a):
        return pl.BlockSpec((tm, a.shape[1]), lambda j, i: (i, 0))

    def wcol(w):
        return pl.BlockSpec((w.shape[0], tn), lambda j, i: (0, j))

    return pl.pallas_call(
        _merge_body,
        grid=(n_branch, m // tm),
        in_specs=[
            act(h), act(act_a), act(att),
            pl.BlockSpec((tm, group), lambda j, i: (i, j)),
            act(act_d),
            pl.BlockSpec((n_branch, d, tn), lambda j, i: (0, 0, j)),
            wcol(lw["w_a_out"]), wcol(lw["w_mla_out"]),
            pl.BlockSpec((None, group, tn), lambda j, i: (j, 0, 0)),
            pl.BlockSpec((1, tn), lambda j, i: (0, j)),
            wcol(lw["w_d_out"]),
        ],
        out_specs=pl.BlockSpec((tm, tn), lambda j, i: (i, j)),
        out_shape=jax.ShapeDtypeStruct((m, d), BF16),
        compiler_params=_cparams("arbitrary", "arbitrary"),
        name="merge",
    )(h, act_a, att, act_p, act_d, lw["wg"], lw["w_a_out"], lw["w_mla_out"], lw["w_pool"],
      lw["pool_scale"], lw["w_d_out"])


def _rope_perm(rope_dim):
    nf = rope_dim // 4
    j = jnp.arange(rope_dim)
    return jnp.where((j % (2 * nf)) < nf, j + nf, j - nf)


def _rope_tables(seq_len, rope_dim):
    rows = seq_len // GRID_W
    row = jnp.broadcast_to(jnp.arange(rows)[:, None], (rows, GRID_W)).reshape(seq_len)
    col = jnp.broadcast_to(jnp.arange(GRID_W)[None, :], (rows, GRID_W)).reshape(seq_len)
    nf = rope_dim // 4
    inv = ROPE_THETA ** (-jnp.arange(nf, dtype=F32) / nf)
    ang = jnp.stack([row, col], axis=-1).astype(F32)[:, :, None] * inv
    cos, sin = jnp.cos(ang), jnp.sin(ang)
    cos_t = jnp.concatenate([cos, cos], axis=-1).reshape(seq_len, rope_dim)
    sin_t = jnp.concatenate([-sin, sin], axis=-1).reshape(seq_len, rope_dim)
    pad = jnp.zeros((seq_len, LANE - rope_dim), F32)
    return jnp.concatenate([cos_t, pad], axis=-1), jnp.concatenate([sin_t, pad], axis=-1)


def _layer_weights(l, p, dims):
    d, wa, wc, wd = dims["d"], dims["wa"], dims["wc"], dims["wd"]
    heads, nope, rope, vdim = dims["heads"], dims["nope"], dims["rope"], dims["vdim"]
    q_lora, kv_lora, nq = dims["q_lora"], dims["kv_lora"], dims["nq"]
    perm = _rope_perm(rope)
    w_in = p["w_in"][l]
    off_q = 3 * wa
    off_kv = off_q + q_lora
    off_kr = off_kv + kv_lora
    off_p = off_kr + rope
    off_d = off_p + wc
    off_g = off_d + 2 * wd
    kr = w_in[:, off_kr:off_p]
    qkv_used = q_lora + kv_lora + 2 * rope
    w_z = jnp.concatenate([
        w_in[:, off_d:off_g],
        w_in[:, wa:3 * wa],
        w_in[:, :wa],
        w_in[:, off_p:off_d],
        w_in[:, off_q:off_kr],
        kr, kr[:, perm],
        jnp.zeros((d, nq - qkv_used), w_in.dtype),
    ], axis=1).astype(BF16)
    n_branch = (w_in.shape[1] - off_g) // d
    wg = w_in[:, off_g:].reshape(d, n_branch, d).transpose(1, 0, 2).astype(BF16)

    qk = nope + rope
    wq = p["w_q_up"][l].reshape(q_lora, heads, qk)
    wq = jnp.concatenate([wq, wq[:, :, nope:][:, :, perm]], axis=-1).reshape(q_lora, heads * 2 * LANE)
    wkv = p["w_kv_up"][l].reshape(kv_lora, heads, nope + vdim)
    wkv = jnp.concatenate([wkv[:, :, :nope].reshape(kv_lora, heads * nope),
                           wkv[:, :, nope:].reshape(kv_lora, heads * vdim)], axis=1)

    def head_gain(g):
        return jnp.concatenate([g, g[nope:][perm]]).reshape(1, 2 * LANE)

    return {
        "w_z": w_z, "wg": wg,
        "wq": wq.astype(BF16), "wkv": wkv.astype(BF16),
        "q_norm_g": p["q_norm_g"][l].reshape(1, q_lora), "kv_norm_g": p["kv_norm_g"][l].reshape(1, kv_lora),
        "q_head_g": head_gain(p["q_head_g"][l]), "k_head_g": head_gain(p["k_head_g"][l]),
        "w_a_out": p["w_a_out"][l].astype(BF16), "w_mla_out": p["w_mla_out"][l].astype(BF16),
        "w_pool": p["w_pool"][l].astype(BF16), "pool_scale": p["pool_scale"][l].reshape(1, d),
        "w_d_out": p["w_d_out"][l].astype(BF16), "w_out": p["w_out"][l].astype(BF16),
        "w_up": _ffn_up_weight(p["w_up"][l], _ffn_chunk(p["w_down"].shape[1])).astype(BF16),
        "w_down": p["w_down"][l].astype(BF16),
    }


def _token_mixer_inputs(xs, mods, rows_per_mod, seq_len, l, p, lw, dims, tables, kv_only=False):
    w_z = lw["w_z"][:, -dims["nq"]:] if kv_only else lw["w_z"]
    z, h = _norm_mm(xs, p["norm1_g"][l], mods, 0, rows_per_mod, w_z, tn=1536)
    q, k, v = _qkv_prep(z, lw, tables[0], tables[1], seq_len, dims, tables[2])
    return z, h, q, k, v


def _finish_layer(xs, mods, rows_per_mod, seq_len, l, p, lw, dims, z, h, att):
    wa, wc, wd = dims["wa"], dims["wc"], dims["wd"]
    act_a = _conv_a(z, p["conv_a_w"][l], wa, seq_len)
    act_p = _pool(z, wc, seq_len)
    act_d = _conv_d(z, p["conv_d_w"][l], p["conv_d_b"][l], p["cd_ln_g"][l], p["cd_ln_b"][l], wd, seq_len)
    merged = _merge(h, act_a, att, act_p, act_d, lw, seq_len)
    x1 = _mm_res(merged, lw["w_out"], xs, mods, 2, rows_per_mod, tm=1024, tn=1024)
    f = _ffn_up(x1, p["norm2_g"][l], mods, 3, rows_per_mod, seq_len, lw["w_up"], p["conv_ff_w"][l],
                p["conv_ff_b"][l])
    return _mm_res(f, lw["w_down"], x1, mods, 5, rows_per_mod, tm=512, tn=1024)


def kernel(x, c, ctx, c_ctx, ada_w, ada_b, norm1_g, w_in, conv_a_w, w_a_out, q_norm_g, w_q_up, kv_norm_g, w_kv_up, q_head_g, k_head_g, w_mla_out, w_pool, pool_scale, conv_d_w, conv_d_b, cd_ln_g, cd_ln_b, w_d_out, w_out, norm2_g, w_up, conv_ff_w, conv_ff_b, w_down):
    p = dict(norm1_g=norm1_g, w_in=w_in, conv_a_w=conv_a_w, w_a_out=w_a_out, q_norm_g=q_norm_g, w_q_up=w_q_up,
             kv_norm_g=kv_norm_g, w_kv_up=w_kv_up, q_head_g=q_head_g, k_head_g=k_head_g, w_mla_out=w_mla_out,
             w_pool=w_pool, pool_scale=pool_scale, conv_d_w=conv_d_w, conv_d_b=conv_d_b, cd_ln_g=cd_ln_g,
             cd_ln_b=cd_ln_b, w_d_out=w_d_out, w_out=w_out, norm2_g=norm2_g, w_up=w_up, conv_ff_w=conv_ff_w,
             conv_ff_b=conv_ff_b, w_down=w_down)
    batch, seq, d = x.shape
    n_ctx = ctx.shape[1]
    depth = ada_w.shape[0]
    qk_dim = q_head_g.shape[1]
    heads = w_q_up.shape[2] // qk_dim
    vdim = w_mla_out.shape[1] // heads
    nope = w_kv_up.shape[2] // heads - vdim
    rope = qk_dim - nope
    wa, wc, wd = conv_a_w.shape[2], w_pool.shape[1] * w_pool.shape[2], conv_d_w.shape[2]
    q_lora, kv_lora = w_q_up.shape[1], w_kv_up.shape[1]
    assert nope == LANE and vdim == LANE and 2 * rope == LANE, "head layout assumes 128 | 64 | 128 dims"
    assert wa == wc == wd and wa % (len(POOL_WINDOWS) * LANE) == 0
    nq = q_lora + kv_lora + 2 * rope
    while (6 * wa) % nq or nq % LANE:
        nq += LANE
    dims = dict(d=d, wa=wa, wc=wc, wd=wd, heads=heads, nope=nope, rope=rope, vdim=vdim, qk_dim=qk_dim,
                q_lora=q_lora, kv_lora=kv_lora, nq=nq)

    rows = -(-(batch + 1) // 8) * 8
    cc = jnp.zeros((rows, d), F32).at[:batch].set(c).at[batch].set(c_ctx)
    mods = _ada(cc, ada_w, ada_b)

    cos_t, sin_t = _rope_tables(seq, rope)
    ones_t = jnp.concatenate([jnp.ones((n_ctx, rope), F32), jnp.zeros((n_ctx, LANE - rope), F32)], axis=-1)
    zeros_t = jnp.zeros((n_ctx, LANE), F32)
    tab_x = (cos_t, sin_t, True)
    tab_c = (ones_t, zeros_t, False)

    xs = x.reshape(batch * seq, d)
    cs = ctx.reshape(batch * n_ctx, d)
    for l in range(depth):
        last = l == depth - 1
        lw = _layer_weights(l, p, dims)
        mods_x = mods[l, :batch].reshape(batch, 1, -1)
        mods_c = mods[l, batch:batch + 1].reshape(1, 1, -1)
        zc, hc, qc, kc, vc = _token_mixer_inputs(cs, mods_c, batch * n_ctx, n_ctx, l, p, lw, dims, tab_c,
                                                 kv_only=last)
        zx, hx, qx, kx, vx = _token_mixer_inputs(xs, mods_x, seq, seq, l, p, lw, dims, tab_x)
        att_x = _attention(qx, kc, vc, kx, vx, batch, heads)
        xs_new = _finish_layer(xs, mods_x, seq, seq, l, p, lw, dims, zx, hx, att_x)
        if not last:
            att_c = _attention(qc, kc, vc, None, None, batch, heads)
            cs = _finish_layer(cs, mods_c, batch * n_ctx, n_ctx, l, p, lw, dims, zc, hc, att_c)
        xs = xs_new
    return xs.reshape(batch, seq, d)
```

```python
import functools

import jax
import jax.numpy as jnp
from jax import lax
from jax.experimental import pallas as pl
from jax.experimental.pallas import tpu as pltpu

GRID_W = 64
ROPE_THETA = 10000.0
POOL_WINDOWS = (2, 4, 8, 16)
EPS = 1e-6
LN_EPS = 1e-5

LOG2E = 1.4426950408889634
LANE = 128
SUBLANES = 8
HALO = 16
VMEM_LIMIT = 56 * 1024 * 1024

F32 = jnp.float32
BF16 = jnp.bfloat16


def _cparams(*sem):
    return pltpu.CompilerParams(dimension_semantics=sem, vmem_limit_bytes=VMEM_LIMIT)


def _pick(n, pref):
    if n <= pref:
        return n
    t = pref
    while n % t:
        t //= 2
    return t


def _ada_body(cc_ref, w_ref, b_ref, o_ref):
    cc = cc_ref[...]
    s = (cc * jax.nn.sigmoid(cc)).astype(BF16)
    o_ref[...] = jnp.dot(s, w_ref[...].astype(BF16), preferred_element_type=F32) + b_ref[...]


def _ada(cc, ada_w, ada_b):
    depth, d, n = ada_w.shape
    rows = cc.shape[0]
    tn = _pick(n, 1024)
    return pl.pallas_call(
        _ada_body,
        grid=(depth, n // tn),
        in_specs=[
            pl.BlockSpec((rows, d), lambda l, j: (0, 0)),
            pl.BlockSpec((None, d, tn), lambda l, j: (l, 0, j)),
            pl.BlockSpec((None, 1, tn), lambda l, j: (l, 0, j)),
        ],
        out_specs=pl.BlockSpec((None, rows, tn), lambda l, j: (l, 0, j)),
        out_shape=jax.ShapeDtypeStruct((depth, rows, n), F32),
        compiler_params=_cparams("arbitrary", "arbitrary"),
        name="ada",
    )(cc, ada_w, ada_b.reshape(depth, 1, n))


def _norm_mm_body(x_ref, g_ref, sh_ref, sc_ref, w_ref, z_ref, h_ref):
    @pl.when(pl.program_id(1) == 0)
    def _():
        x = x_ref[...]
        ms = jnp.mean(x * x, axis=-1, keepdims=True)
        y = x * lax.rsqrt(ms + EPS) * g_ref[...]
        h_ref[...] = (y * (1.0 + sc_ref[...]) + sh_ref[...]).astype(h_ref.dtype)

    z_ref[...] = jnp.dot(h_ref[...], w_ref[...], preferred_element_type=F32).astype(z_ref.dtype)


def _norm_mm(x, gain, mods, k_shift, rows_per_mod, w, *, tn):
    m, d = x.shape
    n = w.shape[1]
    tm = _pick(rows_per_mod, 1024)
    tn = _pick(n, tn)
    in_specs = [
        pl.BlockSpec((tm, d), lambda i, j: (i, 0)),
        pl.BlockSpec((1, d), lambda i, j: (0, 0)),
        pl.BlockSpec((None, 1, d), lambda i, j: (i * tm // rows_per_mod, 0, k_shift)),
        pl.BlockSpec((None, 1, d), lambda i, j: (i * tm // rows_per_mod, 0, k_shift + 1)),
        pl.BlockSpec((d, tn), lambda i, j: (0, j)),
    ]
    return pl.pallas_call(
        _norm_mm_body,
        grid=(m // tm, n // tn),
        in_specs=in_specs,
        out_specs=[pl.BlockSpec((tm, tn), lambda i, j: (i, j)), pl.BlockSpec((tm, d), lambda i, j: (i, 0))],
        out_shape=[jax.ShapeDtypeStruct((m, n), BF16), jax.ShapeDtypeStruct((m, d), BF16)],
        compiler_params=_cparams("arbitrary", "arbitrary"),
        name="norm_mm",
    )(x, gain.reshape(1, d), mods, mods, w)


def _mm_res_body(a_ref, w_ref, r_ref, gate_ref, o_ref):
    acc = jnp.dot(a_ref[...], w_ref[...], preferred_element_type=F32)
    o_ref[...] = r_ref[...] + gate_ref[...] * acc


def _mm_res(a, w, res, mods, k_gate, rows_per_mod, *, tm, tn):
    m, k = a.shape
    n = w.shape[1]
    tm = _pick(rows_per_mod, tm)
    tn = _pick(n, tn)
    nj = n // tn
    return pl.pallas_call(
        _mm_res_body,
        grid=(m // tm, nj),
        in_specs=[
            pl.BlockSpec((tm, k), lambda i, j: (i, 0)),
            pl.BlockSpec((k, tn), lambda i, j: (0, j)),
            pl.BlockSpec((tm, tn), lambda i, j: (i, j)),
            pl.BlockSpec((None, 1, tn), lambda i, j: (i * tm // rows_per_mod, 0, k_gate * nj + j)),
        ],
        out_specs=pl.BlockSpec((tm, tn), lambda i, j: (i, j)),
        out_shape=jax.ShapeDtypeStruct((m, n), F32),
        compiler_params=_cparams("arbitrary", "arbitrary"),
        name="mm_res",
    )(a, w, res, mods)


def _halo_specs(tl, width, col_block, n_rows):
    per = tl // HALO
    last = n_rows // HALO - 1
    prev = pl.BlockSpec((HALO, width), lambda i, *_: (jnp.maximum(i * per - 1, 0), col_block))
    nxt = pl.BlockSpec((HALO, width), lambda i, *_: (jnp.minimum((i + 1) * per, last), col_block))
    return prev, nxt


def _seq_edges(tps):
    s = pl.program_id(0) % tps
    return s, s == 0, s == tps - 1


def _fill_lhs(lhs_ref, hp_ref, h_ref, hn_ref, first, last, tm):
    zero = jnp.zeros((HALO, lhs_ref.shape[1]), lhs_ref.dtype)
    lhs_ref[0:HALO, :] = jnp.where(first, zero, hp_ref[...])
    lhs_ref[HALO:HALO + tm, :] = h_ref[...]
    lhs_ref[HALO + tm:HALO + tm + HALO, :] = jnp.where(last, zero, hn_ref[...])


def _h_specs(tm, d, m):
    prev, nxt = _halo_specs(tm, d, 0, m)
    return [pl.BlockSpec((tm, d), lambda i, *_: (i, 0)), prev, nxt]


def _proj_conv_a_body(h_ref, hp_ref, hn_ref, wcg_ref, whh_ref, wb_ref, cw_ref, o_ref, lhs_ref, *, tm, tps):
    _, first, last = _seq_edges(tps)

    @pl.when(pl.program_id(1) == 0)
    def _():
        _fill_lhs(lhs_ref, hp_ref, h_ref, hn_ref, first, last, tm)

    lhs = lhs_ref[...]
    u = (jnp.dot(lhs, wcg_ref[...], preferred_element_type=F32)
         * jnp.dot(lhs, whh_ref[...], preferred_element_type=F32))
    b = jnp.dot(lhs_ref[HALO:HALO + tm, :], wb_ref[...], preferred_element_type=F32)
    taps = cw_ref.shape[0]
    pad = taps // 2
    conv = cw_ref[0:1, :] * u[HALO - pad:HALO - pad + tm, :]
    for t in range(1, taps):
        conv = conv + cw_ref[t:t + 1, :] * u[HALO - pad + t:HALO - pad + t + tm, :]
    o_ref[...] = (b * conv).astype(o_ref.dtype)


def _proj_conv_a(h, w_z, conv_w, width, seq_len):
    m, d = h.shape
    tm = _pick(seq_len, 1024)
    tc = _pick(width, 512)
    nb = width // tc

    def wcol(group):
        return pl.BlockSpec((d, tc), lambda i, j: (0, group * nb + j))

    return pl.pallas_call(
        functools.partial(_proj_conv_a_body, tm=tm, tps=seq_len // tm),
        grid=(m // tm, nb),
        in_specs=_h_specs(tm, d, m) + [wcol(2), wcol(3), wcol(4),
                                       pl.BlockSpec((conv_w.shape[0], tc), lambda i, j: (0, j))],
        out_specs=pl.BlockSpec((tm, tc), lambda i, j: (i, j)),
        out_shape=jax.ShapeDtypeStruct((m, width), BF16),
        scratch_shapes=[pltpu.VMEM((tm + 2 * HALO, d), BF16)],
        compiler_params=_cparams("arbitrary", "arbitrary"),
        name="proj_conv_a",
    )(h, h, h, w_z, w_z, w_z, conv_w)


def _window_sum(ext, win, tm):
    half = win // 2
    part, length, span = ext, ext.shape[0], 1
    while span < half:
        length -= SUBLANES
        part = part[0:length, :] + part[span:span + length, :]
        span *= 2
    return part[HALO - half:HALO - half + tm, :] + part[HALO:HALO + tm, :]


def _proj_pool_body(h_ref, hp_ref, hn_ref, w_ref, o_ref, lhs_ref, *, tm, tps, seq_len, group):
    s, first, last = _seq_edges(tps)
    _fill_lhs(lhs_ref, hp_ref, h_ref, hn_ref, first, last, tm)
    u = jnp.dot(lhs_ref[...], w_ref[...], preferred_element_type=F32)
    u = jnp.concatenate([u, jnp.zeros((SUBLANES, u.shape[1]), F32)], axis=0)
    pos = s * tm + lax.broadcasted_iota(jnp.int32, (tm, 1), 0)
    for g, win in enumerate(POOL_WINDOWS):
        cols = slice(g * group, (g + 1) * group)
        ext = u[:, cols]
        lo = jnp.maximum(pos - win // 2, 0)
        hi = jnp.minimum(pos - win // 2 + win, seq_len)
        cnt = (hi - lo).astype(F32)
        o_ref[:, cols] = (_window_sum(ext, win, tm) / cnt - ext[HALO:HALO + tm, :]).astype(o_ref.dtype)


def _proj_pool(h, w_z, width, seq_len):
    m, d = h.shape
    assert max(POOL_WINDOWS) // 2 <= SUBLANES and max(POOL_WINDOWS) <= HALO
    tm = _pick(seq_len, 512)
    return pl.pallas_call(
        functools.partial(_proj_pool_body, tm=tm, tps=seq_len // tm, seq_len=seq_len,
                          group=width // len(POOL_WINDOWS)),
        grid=(m // tm,),
        in_specs=_h_specs(tm, d, m) + [pl.BlockSpec((d, width), lambda i: (0, 5))],
        out_specs=pl.BlockSpec((tm, width), lambda i: (i, 0)),
        out_shape=jax.ShapeDtypeStruct((m, width), BF16),
        scratch_shapes=[pltpu.VMEM((tm + 2 * HALO, d), BF16)],
        compiler_params=_cparams("arbitrary"),
        name="proj_pool",
    )(h, h, h, w_z)


def _proj_conv_d_body(h_ref, hp_ref, hn_ref, wa_ref, wgt_ref, w_ref, cb_ref, lg_ref, lb_ref, o_ref,
                      lhs_ref, ext_ref, sh_ref, *, tl, tps):
    _, first, last = _seq_edges(tps)
    _fill_lhs(lhs_ref, hp_ref, h_ref, hn_ref, first, last, tl)
    lhs = lhs_ref[...]
    rows = sh_ref.shape[1]
    taps = w_ref.shape[0]
    pad = taps // 2
    slab = 2 * LANE
    ys = []
    for c0 in range(0, w_ref.shape[1], slab):
        cols = slice(c0, c0 + slab)
        ext_ref[:, cols] = (jnp.dot(lhs, wa_ref[:, cols], preferred_element_type=F32)
                            * jax.nn.sigmoid(jnp.dot(lhs, wgt_ref[:, cols], preferred_element_type=F32)))
        for r in range(SUBLANES - 1):
            sh_ref[r, :, cols] = ext_ref[r + 1:r + 1 + rows, cols]
        acc = None
        for k in range(taps):
            off = HALO - pad + k
            if off % SUBLANES == 0:
                tap = ext_ref[off:off + tl, cols]
            else:
                base = off - off % SUBLANES
                tap = sh_ref[off % SUBLANES - 1, base:base + tl, cols]
            term = w_ref[k:k + 1, cols] * tap
            acc = term if acc is None else acc + term
        ys.append(acc)
    y = jnp.concatenate(ys, axis=1) + cb_ref[...]
    mu = jnp.mean(y, axis=-1, keepdims=True)
    yc = y - mu
    var = jnp.mean(yc * yc, axis=-1, keepdims=True)
    yn = yc * lax.rsqrt(var + LN_EPS) * lg_ref[...] + lb_ref[...]
    o_ref[...] = (yn * jax.nn.sigmoid(yn)).astype(o_ref.dtype)


def _proj_conv_d(h, w_z, conv_w, conv_b, ln_g, ln_b, width, seq_len):
    m, d = h.shape
    tl = _pick(seq_len, 256)
    vec = pl.BlockSpec((1, width), lambda i: (0, 0))
    return pl.pallas_call(
        functools.partial(_proj_conv_d_body, tl=tl, tps=seq_len // tl),
        grid=(m // tl,),
        in_specs=_h_specs(tl, d, m) + [
            pl.BlockSpec((d, width), lambda i: (0, 0)),
            pl.BlockSpec((d, width), lambda i: (0, 1)),
            pl.BlockSpec(conv_w.shape, lambda i: (0, 0)),
            vec, vec, vec,
        ],
        out_specs=pl.BlockSpec((tl, width), lambda i: (i, 0)),
        out_shape=jax.ShapeDtypeStruct((m, width), BF16),
        scratch_shapes=[pltpu.VMEM((tl + 2 * HALO, d), BF16),
                        pltpu.VMEM((tl + 2 * HALO, width), F32),
                        pltpu.VMEM((SUBLANES - 1, tl + 2 * HALO - SUBLANES, width), F32)],
        compiler_params=_cparams("arbitrary"),
        name="proj_conv_d",
    )(h, h, h, w_z, w_z, conv_w, conv_b.reshape(1, width), ln_g.reshape(1, width), ln_b.reshape(1, width))


def _ffn_up_body(x_ref, xp_ref, xn_ref, g_ref, sh_ref, sc_ref, wg_ref, wv_ref, cw_ref, cb_ref, f_ref, h_ref,
                 *, tm, tps):
    _, first, last = _seq_edges(tps)

    @pl.when(pl.program_id(1) == 0)
    def _():
        def norm(x):
            ms = jnp.mean(x * x, axis=-1, keepdims=True)
            y = x * lax.rsqrt(ms + EPS) * g_ref[...]
            return (y * (1.0 + sc_ref[...]) + sh_ref[...]).astype(h_ref.dtype)

        zero = jnp.zeros((HALO, h_ref.shape[1]), h_ref.dtype)
        h_ref[0:HALO, :] = jnp.where(first, zero, norm(xp_ref[...]))
        h_ref[HALO:HALO + tm, :] = norm(x_ref[...])
        h_ref[HALO + tm:HALO + tm + HALO, :] = jnp.where(last, zero, norm(xn_ref[...]))

    gate = jnp.dot(h_ref[...], wg_ref[...], preferred_element_type=F32)
    val = jnp.dot(h_ref[HALO:HALO + tm, :], wv_ref[...], preferred_element_type=F32)
    taps = cw_ref.shape[0]
    pad = taps // 2
    conv = cw_ref[0:1, :] * gate[HALO - pad:HALO - pad + tm, :]
    for t in range(1, taps):
        conv = conv + cw_ref[t:t + 1, :] * gate[HALO - pad + t:HALO - pad + t + tm, :]
    g = conv + cb_ref[...]
    f_ref[...] = (g * jax.nn.sigmoid(g) * val).astype(f_ref.dtype)


def _ffn_up(x, gain, mods, k_shift, rows_per_mod, seq_len, w_up, conv_w, conv_b):
    m, d = x.shape
    d_ff = conv_w.shape[1]
    tc = _pick(d_ff, 512)
    nj = d_ff // tc
    tm = _pick(seq_len, 1024)
    per = tm // HALO
    last_blk = m // HALO - 1
    return pl.pallas_call(
        functools.partial(_ffn_up_body, tm=tm, tps=seq_len // tm),
        grid=(m // tm, nj),
        in_specs=[
            pl.BlockSpec((tm, d), lambda i, j: (i, 0)),
            pl.BlockSpec((HALO, d), lambda i, j: (jnp.maximum(i * per - 1, 0), 0)),
            pl.BlockSpec((HALO, d), lambda i, j: (jnp.minimum((i + 1) * per, last_blk), 0)),
            pl.BlockSpec((1, d), lambda i, j: (0, 0)),
            pl.BlockSpec((None, 1, d), lambda i, j: (i * tm // rows_per_mod, 0, k_shift)),
            pl.BlockSpec((None, 1, d), lambda i, j: (i * tm // rows_per_mod, 0, k_shift + 1)),
            pl.BlockSpec((d, tc), lambda i, j: (0, j)),
            pl.BlockSpec((d, tc), lambda i, j: (0, nj + j)),
            pl.BlockSpec((conv_w.shape[0], tc), lambda i, j: (0, j)),
            pl.BlockSpec((1, tc), lambda i, j: (0, j)),
        ],
        out_specs=pl.BlockSpec((tm, tc), lambda i, j: (i, j)),
        out_shape=jax.ShapeDtypeStruct((m, d_ff), BF16),
        scratch_shapes=[pltpu.VMEM((tm + 2 * HALO, d), BF16)],
        compiler_params=_cparams("arbitrary", "arbitrary"),
        name="ffn_up",
    )(x, x, x, gain.reshape(1, d), mods, mods, w_up, w_up, conv_w, conv_b.reshape(1, d_ff))


def _rope_vreg(bn, cos_ref, sin_ref):
    return bn * cos_ref[...] + pltpu.roll(bn, LANE // 2, 1) * sin_ref[...]


def _qkv_prep_body(z_ref, qg_ref, wq_ref, kvg_ref, wkv_ref, qhg_ref, khg_ref, cos_ref, sin_ref,
                   q_ref, k_ref, v_ref, *, q_lora, kv_lora, heads, qk_dim, scale):
    z = z_ref[...].astype(F32)
    lane = lax.broadcasted_iota(jnp.int32, (1, LANE), 1)
    rope_lanes = lane < (LANE // 2)
    slot = 2 * LANE

    def rms(v, g):
        return v * lax.rsqrt(jnp.mean(v * v, axis=-1, keepdims=True) + EPS) * g

    qn = rms(z[:, :q_lora], qg_ref[...]).astype(BF16)
    yq = jnp.dot(qn, wq_ref[...], preferred_element_type=F32)
    g_nope, g_rope = qhg_ref[:, :LANE], qhg_ref[:, LANE:]
    for h in range(heads):
        a = yq[:, h * slot:h * slot + LANE]
        b = yq[:, h * slot + LANE:(h + 1) * slot]
        ss = jnp.sum(a * a + jnp.where(rope_lanes, b * b, 0.0), axis=-1, keepdims=True)
        r = lax.rsqrt(ss / qk_dim + EPS) * scale
        q_ref[:, h * slot:h * slot + LANE] = (a * r * g_nope).astype(q_ref.dtype)
        q_ref[:, h * slot + LANE:(h + 1) * slot] = _rope_vreg(b * r * g_rope, cos_ref, sin_ref).astype(q_ref.dtype)

    kvn = rms(z[:, q_lora:q_lora + kv_lora], kvg_ref[...]).astype(BF16)
    ykv = jnp.dot(kvn, wkv_ref[...], preferred_element_type=F32)
    kr = z[:, q_lora + kv_lora:q_lora + kv_lora + LANE]
    kr_ss = jnp.sum(jnp.where(rope_lanes, kr * kr, 0.0), axis=-1, keepdims=True)
    g_nope, g_rope = khg_ref[:, :LANE], khg_ref[:, LANE:]
    for h in range(heads):
        a = ykv[:, h * LANE:(h + 1) * LANE]
        ss = jnp.sum(a * a, axis=-1, keepdims=True) + kr_ss
        r = lax.rsqrt(ss / qk_dim + EPS)
        k_ref[:, h * slot:h * slot + LANE] = (a * r * g_nope).astype(k_ref.dtype)
        k_ref[:, h * slot + LANE:(h + 1) * slot] = _rope_vreg(kr * r * g_rope, cos_ref, sin_ref).astype(k_ref.dtype)
    ones = jnp.ones((z.shape[0], LANE), v_ref.dtype)
    for h in range(heads):
        v_ref[:, h * slot:h * slot + LANE] = ykv[:, (heads + h) * LANE:(heads + h + 1) * LANE].astype(v_ref.dtype)
        v_ref[:, h * slot + LANE:(h + 1) * slot] = ones


def _qkv_prep(z, lw, cos_t, sin_t, seq_len, dims, use_rope):
    m = z.shape[0]
    heads, q_lora, kv_lora, qk_dim = dims["heads"], dims["q_lora"], dims["kv_lora"], dims["qk_dim"]
    nq = dims["nq"]
    tm = _pick(seq_len, 512)
    tps = seq_len // tm
    col_block = (z.shape[1] - nq) // nq
    if use_rope:
        tab = pl.BlockSpec((tm, LANE), lambda i: (i % tps, 0))
    else:
        tab = pl.BlockSpec((tm, LANE), lambda i: (0, 0))

    def full(a):
        return pl.BlockSpec(a.shape, lambda i: (0, 0))

    qw = heads * 2 * LANE
    return pl.pallas_call(
        functools.partial(_qkv_prep_body, q_lora=q_lora, kv_lora=kv_lora, heads=heads, qk_dim=qk_dim,
                          scale=qk_dim ** -0.5 * LOG2E),
        grid=(m // tm,),
        in_specs=[
            pl.BlockSpec((tm, nq), lambda i: (i, col_block)),
            full(lw["q_norm_g"]), full(lw["wq"]), full(lw["kv_norm_g"]), full(lw["wkv"]),
            full(lw["q_head_g"]), full(lw["k_head_g"]), tab, tab,
        ],
        out_specs=[
            pl.BlockSpec((tm, qw), lambda i: (i, 0)),
            pl.BlockSpec((tm, qw), lambda i: (i, 0)),
            pl.BlockSpec((tm, qw), lambda i: (i, 0)),
        ],
        out_shape=[
            jax.ShapeDtypeStruct((m, qw), BF16),
            jax.ShapeDtypeStruct((m, qw), BF16),
            jax.ShapeDtypeStruct((m, qw), BF16),
        ],
        compiler_params=_cparams("arbitrary"),
        name="qkv_prep",
    )(z, lw["q_norm_g"], lw["wq"], lw["kv_norm_g"], lw["wkv"], lw["q_head_g"], lw["k_head_g"], cos_t, sin_t)


def _attn_update(q, k, v1, carry):
    m_i, acc = carry
    s = lax.dot_general(q, k, (((1,), (1,)), ((), ())), preferred_element_type=F32)
    m_new = jnp.maximum(m_i, jnp.max(s, axis=-1, keepdims=True))
    alpha = jnp.exp2(m_i - m_new)
    p = jnp.exp2(s - m_new)
    acc = alpha * acc + jnp.dot(p.astype(v1.dtype), v1, preferred_element_type=F32)
    return m_new, acc


def _attn_body(*refs, tk, n_lat):
    if n_lat:
        q_ref, kc_ref, vc_ref, kx_ref, vx_ref, o_ref = refs
    else:
        q_ref, kc_ref, vc_ref, o_ref = refs
    q = q_ref[...]
    tq = q.shape[0]
    vd = o_ref.shape[-1]
    carry = (jnp.full((tq, 1), -jnp.inf, F32), jnp.zeros((tq, 2 * vd), F32))
    carry = _attn_update(q, kc_ref[...], vc_ref[...], carry)
    for c in range(n_lat):
        carry = _attn_update(q, kx_ref[c * tk:(c + 1) * tk, :], vx_ref[c * tk:(c + 1) * tk, :], carry)
    _, acc = carry
    o_ref[...] = (acc[:, :vd] / acc[:, vd:]).astype(o_ref.dtype)


def _attention(q, kc, vc, kx, vx, batch, heads):
    lq = q.shape[0] // batch
    lc = kc.shape[0] // batch
    slot, vd = 2 * LANE, LANE
    tq = _pick(lq, 1024)
    q3 = q.reshape(batch, lq, heads * slot)
    args = [q3, kc.reshape(batch, lc, heads * slot), vc.reshape(batch, lc, heads * slot)]
    in_specs = [
        pl.BlockSpec((None, tq, slot), lambda b, h, i: (b, i, h)),
        pl.BlockSpec((None, lc, slot), lambda b, h, i: (b, 0, h)),
        pl.BlockSpec((None, lc, slot), lambda b, h, i: (b, 0, h)),
    ]
    n_lat, tk = 0, 0
    if kx is not None:
        lx = kx.shape[0] // batch
        tk = _pick(lx, 1024)
        n_lat = lx // tk
        args += [kx.reshape(batch, lx, heads * slot), vx.reshape(batch, lx, heads * slot)]
        in_specs += [
            pl.BlockSpec((None, lx, slot), lambda b, h, i: (b, 0, h)),
            pl.BlockSpec((None, lx, slot), lambda b, h, i: (b, 0, h)),
        ]
    out = pl.pallas_call(
        functools.partial(_attn_body, tk=tk, n_lat=n_lat),
        grid=(batch, heads, lq // tq),
        in_specs=in_specs,
        out_specs=pl.BlockSpec((None, tq, vd), lambda b, h, i: (b, i, h)),
        out_shape=jax.ShapeDtypeStruct((batch, lq, heads * vd), BF16),
        compiler_params=_cparams("arbitrary", "arbitrary", "arbitrary"),
        name="attention_x" if n_lat else "attention_c",
    )(*args)
    return out.reshape(batch * lq, heads * vd)


def _merge_body(h_ref, a_ref, t_ref, p_ref, d_ref, wg_ref, wa_ref, wt_ref, wp_ref, ps_ref, wd_ref, o_ref):
    h = h_ref[...]

    def gate(i):
        return jax.nn.sigmoid(jnp.dot(h, wg_ref[i], preferred_element_type=F32))

    def proj(x_ref, w):
        return jnp.dot(x_ref[...], w, preferred_element_type=F32)

    acc = gate(0) * proj(a_ref, wa_ref[...])
    acc = acc + gate(1) * proj(t_ref, wt_ref[...])
    acc = acc + gate(2) * (proj(p_ref, wp_ref[...]) * ps_ref[...])
    acc = acc + gate(3) * proj(d_ref, wd_ref[...])
    o_ref[...] = acc.astype(o_ref.dtype)


def _merge(h, act_a, att, act_p, act_d, lw, rows_per_seq):
    m, d = h.shape
    n_branch = lw["wg"].shape[0]
    tn = d // n_branch
    group = act_p.shape[1] // n_branch
    tm = _pick(rows_per_seq, 512)

    def act(a):
        return pl.BlockSpec((tm, a.shape[1]), lambda j, i: (i, 0))

    def wcol(w):
        return pl.BlockSpec((w.shape[0], tn), lambda j, i: (0, j))

    return pl.pallas_call(
        _merge_body,
        grid=(n_branch, m // tm),
        in_specs=[
            act(h), act(act_a), act(att),
            pl.BlockSpec((tm, group), lambda j, i: (i, j)),
            act(act_d),
            pl.BlockSpec((n_branch, d, tn), lambda j, i: (0, 0, j)),
            wcol(lw["w_a_out"]), wcol(lw["w_mla_out"]),
            pl.BlockSpec((None, group, tn), lambda j, i: (j, 0, 0)),
            pl.BlockSpec((1, tn), lambda j, i: (0, j)),
            wcol(lw["w_d_out"]),
        ],
        out_specs=pl.BlockSpec((tm, tn), lambda j, i: (i, j)),
        out_shape=jax.ShapeDtypeStruct((m, d), BF16),
        compiler_params=_cparams("arbitrary", "arbitrary"),
        name="merge",
    )(h, act_a, att, act_p, act_d, lw["wg"], lw["w_a_out"], lw["w_mla_out"], lw["w_pool"],
      lw["pool_scale"], lw["w_d_out"])


def _rope_perm(rope_dim):
    nf = rope_dim // 4
    j = jnp.arange(rope_dim)
    return jnp.where((j % (2 * nf)) < nf, j + nf, j - nf)


def _rope_tables(seq_len, rope_dim):
    rows = seq_len // GRID_W
    row = jnp.broadcast_to(jnp.arange(rows)[:, None], (rows, GRID_W)).reshape(seq_len)
    col = jnp.broadcast_to(jnp.arange(GRID_W)[None, :], (rows, GRID_W)).reshape(seq_len)
    nf = rope_dim // 4
    inv = ROPE_THETA ** (-jnp.arange(nf, dtype=F32) / nf)
    ang = jnp.stack([row, col], axis=-1).astype(F32)[:, :, None] * inv
    cos, sin = jnp.cos(ang), jnp.sin(ang)
    cos_t = jnp.concatenate([cos, cos], axis=-1).reshape(seq_len, rope_dim)
    sin_t = jnp.concatenate([-sin, sin], axis=-1).reshape(seq_len, rope_dim)
    pad = jnp.zeros((seq_len, LANE - rope_dim), F32)
    return jnp.concatenate([cos_t, pad], axis=-1), jnp.concatenate([sin_t, pad], axis=-1)


def _layer_weights(l, p, dims):
    d, wa, wc, wd = dims["d"], dims["wa"], dims["wc"], dims["wd"]
    heads, nope, rope, vdim = dims["heads"], dims["nope"], dims["rope"], dims["vdim"]
    q_lora, kv_lora, nq = dims["q_lora"], dims["kv_lora"], dims["nq"]
    perm = _rope_perm(rope)
    w_in = p["w_in"][l]
    off_q = 3 * wa
    off_kv = off_q + q_lora
    off_kr = off_kv + kv_lora
    off_p = off_kr + rope
    off_d = off_p + wc
    off_g = off_d + 2 * wd
    kr = w_in[:, off_kr:off_p]
    qkv_used = q_lora + kv_lora + 2 * rope
    w_z = jnp.concatenate([
        w_in[:, off_d:off_g],
        w_in[:, wa:3 * wa],
        w_in[:, :wa],
        w_in[:, off_p:off_d],
        w_in[:, off_q:off_kr],
        kr, kr[:, perm],
        jnp.zeros((d, nq - qkv_used), w_in.dtype),
    ], axis=1).astype(BF16)
    n_branch = (w_in.shape[1] - off_g) // d
    wg = w_in[:, off_g:].reshape(d, n_branch, d).transpose(1, 0, 2).astype(BF16)

    qk = nope + rope
    wq = p["w_q_up"][l].reshape(q_lora, heads, qk)
    wq = jnp.concatenate([wq, wq[:, :, nope:][:, :, perm]], axis=-1).reshape(q_lora, heads * 2 * LANE)
    wkv = p["w_kv_up"][l].reshape(kv_lora, heads, nope + vdim)
    wkv = jnp.concatenate([wkv[:, :, :nope].reshape(kv_lora, heads * nope),
                           wkv[:, :, nope:].reshape(kv_lora, heads * vdim)], axis=1)

    def head_gain(g):
        return jnp.concatenate([g, g[nope:][perm]]).reshape(1, 2 * LANE)

    return {
        "w_z": w_z, "wg": wg,
        "wq": wq.astype(BF16), "wkv": wkv.astype(BF16),
        "q_norm_g": p["q_norm_g"][l].reshape(1, q_lora), "kv_norm_g": p["kv_norm_g"][l].reshape(1, kv_lora),
        "q_head_g": head_gain(p["q_head_g"][l]), "k_head_g": head_gain(p["k_head_g"][l]),
        "w_a_out": p["w_a_out"][l].astype(BF16), "w_mla_out": p["w_mla_out"][l].astype(BF16),
        "w_pool": p["w_pool"][l].astype(BF16), "pool_scale": p["pool_scale"][l].reshape(1, d),
        "w_d_out": p["w_d_out"][l].astype(BF16), "w_out": p["w_out"][l].astype(BF16),
        "w_up": p["w_up"][l].astype(BF16), "w_down": p["w_down"][l].astype(BF16),
    }


def _token_mixer_inputs(xs, mods, rows_per_mod, seq_len, l, p, lw, dims, tables):
    z, h = _norm_mm(xs, p["norm1_g"][l], mods, 0, rows_per_mod, lw["w_z"][:, -dims["nq"]:], tn=1536)
    q, k, v = _qkv_prep(z, lw, tables[0], tables[1], seq_len, dims, tables[2])
    return h, q, k, v


def _finish_layer(xs, mods, rows_per_mod, seq_len, l, p, lw, dims, h, att):
    wa, wc, wd = dims["wa"], dims["wc"], dims["wd"]
    act_a = _proj_conv_a(h, lw["w_z"], p["conv_a_w"][l], wa, seq_len)
    act_p = _proj_pool(h, lw["w_z"], wc, seq_len)
    act_d = _proj_conv_d(h, lw["w_z"], p["conv_d_w"][l], p["conv_d_b"][l], p["cd_ln_g"][l], p["cd_ln_b"][l],
                         wd, seq_len)
    merged = _merge(h, act_a, att, act_p, act_d, lw, seq_len)
    x1 = _mm_res(merged, lw["w_out"], xs, mods, 2, rows_per_mod, tm=1024, tn=1024)
    f = _ffn_up(x1, p["norm2_g"][l], mods, 3, rows_per_mod, seq_len, lw["w_up"], p["conv_ff_w"][l],
                p["conv_ff_b"][l])
    return _mm_res(f, lw["w_down"], x1, mods, 5, rows_per_mod, tm=512, tn=1024)


def kernel(x, c, ctx, c_ctx, ada_w, ada_b, norm1_g, w_in, conv_a_w, w_a_out, q_norm_g, w_q_up, kv_norm_g, w_kv_up, q_head_g, k_head_g, w_mla_out, w_pool, pool_scale, conv_d_w, conv_d_b, cd_ln_g, cd_ln_b, w_d_out, w_out, norm2_g, w_up, conv_ff_w, conv_ff_b, w_down):
    p = dict(norm1_g=norm1_g, w_in=w_in, conv_a_w=conv_a_w, w_a_out=w_a_out, q_norm_g=q_norm_g, w_q_up=w_q_up,
             kv_norm_g=kv_norm_g, w_kv_up=w_kv_up, q_head_g=q_head_g, k_head_g=k_head_g, w_mla_out=w_mla_out,
             w_pool=w_pool, pool_scale=pool_scale, conv_d_w=conv_d_w, conv_d_b=conv_d_b, cd_ln_g=cd_ln_g,
             cd_ln_b=cd_ln_b, w_d_out=w_d_out, w_out=w_out, norm2_g=norm2_g, w_up=w_up, conv_ff_w=conv_ff_w,
             conv_ff_b=conv_ff_b, w_down=w_down)
    batch, seq, d = x.shape
    n_ctx = ctx.shape[1]
    depth = ada_w.shape[0]
    qk_dim = q_head_g.shape[1]
    heads = w_q_up.shape[2] // qk_dim
    vdim = w_mla_out.shape[1] // heads
    nope = w_kv_up.shape[2] // heads - vdim
    rope = qk_dim - nope
    wa, wc, wd = conv_a_w.shape[2], w_pool.shape[1] * w_pool.shape[2], conv_d_w.shape[2]
    q_lora, kv_lora = w_q_up.shape[1], w_kv_up.shape[1]
    assert nope == LANE and vdim == LANE and 2 * rope == LANE, "head layout assumes 128 | 64 | 128 dims"
    assert wa == wc == wd and wa % (len(POOL_WINDOWS) * LANE) == 0
    nq = q_lora + kv_lora + 2 * rope
    while (6 * wa) % nq or nq % LANE:
        nq += LANE
    dims = dict(d=d, wa=wa, wc=wc, wd=wd, heads=heads, nope=nope, rope=rope, vdim=vdim, qk_dim=qk_dim,
                q_lora=q_lora, kv_lora=kv_lora, nq=nq)

    rows = -(-(batch + 1) // 8) * 8
    cc = jnp.zeros((rows, d), F32).at[:batch].set(c).at[batch].set(c_ctx)
    mods = _ada(cc, ada_w, ada_b)

    cos_t, sin_t = _rope_tables(seq, rope)
    ones_t = jnp.concatenate([jnp.ones((n_ctx, rope), F32), jnp.zeros((n_ctx, LANE - rope), F32)], axis=-1)
    zeros_t = jnp.zeros((n_ctx, LANE), F32)
    tab_x = (cos_t, sin_t, True)
    tab_c = (ones_t, zeros_t, False)

    xs = x.reshape(batch * seq, d)
    cs = ctx.reshape(batch * n_ctx, d)
    for l in range(depth):
        last = l == depth - 1
        lw = _layer_weights(l, p, dims)
        mods_x = mods[l, :batch].reshape(batch, 1, -1)
        mods_c = mods[l, batch:batch + 1].reshape(1, 1, -1)
        hc, qc, kc, vc = _token_mixer_inputs(cs, mods_c, batch * n_ctx, n_ctx, l, p, lw, dims, tab_c)
        hx, qx, kx, vx = _token_mixer_inputs(xs, mods_x, seq, seq, l, p, lw, dims, tab_x)
        att_x = _attention(qx, kc, vc, kx, vx, batch, heads)
        xs_new = _finish_layer(xs, mods_x, seq, seq, l, p, lw, dims, hx, att_x)
        if not last:
            att_c = _attention(qc, kc, vc, None, None, batch, heads)
            cs = _finish_layer(cs, mods_c, batch * n_ctx, n_ctx, l, p, lw, dims, hc, att_c)
        xs = xs_new
    return xs.reshape(batch, seq, d)
```

```python
import functools

import jax
import jax.numpy as jnp
from jax import lax
from jax.experimental import pallas as pl
from jax.experimental.pallas import tpu as pltpu

GRID_W = 64
ROPE_THETA = 10000.0
POOL_WINDOWS = (2, 4, 8, 16)
EPS = 1e-6
LN_EPS = 1e-5

LOG2E = 1.4426950408889634
LANE = 128
SUBLANES = 8
HALO = 16
VMEM_LIMIT = 56 * 1024 * 1024

F32 = jnp.float32
BF16 = jnp.bfloat16


def _cparams(*sem):
    return pltpu.CompilerParams(dimension_semantics=sem, vmem_limit_bytes=VMEM_LIMIT)


def _sigmoid(x):
    return 0.5 * jnp.tanh(0.5 * x) + 0.5


def _pick(n, pref):
    if n <= pref:
        return n
    t = pref
    while n % t:
        t //= 2
    return t


def _ada_body(cc_ref, w_ref, b_ref, o_ref):
    cc = cc_ref[...]
    s = (cc * jax.nn.sigmoid(cc)).astype(BF16)
    o_ref[...] = jnp.dot(s, w_ref[...].astype(BF16), preferred_element_type=F32) + b_ref[...]


def _ada(cc, ada_w, ada_b):
    depth, d, n = ada_w.shape
    rows = cc.shape[0]
    tn = _pick(n, 1024)
    return pl.pallas_call(
        _ada_body,
        grid=(depth, n // tn),
        in_specs=[
            pl.BlockSpec((rows, d), lambda l, j: (0, 0)),
            pl.BlockSpec((None, d, tn), lambda l, j: (l, 0, j)),
            pl.BlockSpec((None, 1, tn), lambda l, j: (l, 0, j)),
        ],
        out_specs=pl.BlockSpec((None, rows, tn), lambda l, j: (l, 0, j)),
        out_shape=jax.ShapeDtypeStruct((depth, rows, n), F32),
        compiler_params=_cparams("arbitrary", "arbitrary"),
        name="ada",
    )(cc, ada_w, ada_b.reshape(depth, 1, n))


def _mm_res_body(a_ref, w_ref, r_ref, gate_ref, o_ref):
    acc = jnp.dot(a_ref[...], w_ref[...], preferred_element_type=F32)
    o_ref[...] = r_ref[...] + gate_ref[...] * acc


def _mm_res(a, w, res, mods, k_gate, rows_per_mod, *, tm, tn):
    m, k = a.shape
    n = w.shape[1]
    tm = _pick(rows_per_mod, tm)
    tn = _pick(n, tn)
    nj = n // tn
    return pl.pallas_call(
        _mm_res_body,
        grid=(m // tm, nj),
        in_specs=[
            pl.BlockSpec((tm, k), lambda i, j: (i, 0)),
            pl.BlockSpec((k, tn), lambda i, j: (0, j)),
            pl.BlockSpec((tm, tn), lambda i, j: (i, j)),
            pl.BlockSpec((None, 1, tn), lambda i, j: (i * tm // rows_per_mod, 0, k_gate * nj + j)),
        ],
        out_specs=pl.BlockSpec((tm, tn), lambda i, j: (i, j)),
        out_shape=jax.ShapeDtypeStruct((m, n), F32),
        compiler_params=_cparams("arbitrary", "arbitrary"),
        name="mm_res",
    )(a, w, res, mods)


def _halo_specs(tl, width, col_block, n_rows):
    per = tl // HALO
    last = n_rows // HALO - 1
    prev = pl.BlockSpec((HALO, width), lambda i, *_: (jnp.maximum(i * per - 1, 0), col_block))
    nxt = pl.BlockSpec((HALO, width), lambda i, *_: (jnp.minimum((i + 1) * per, last), col_block))
    return prev, nxt


def _seq_edges(tps):
    s = pl.program_id(0) % tps
    return s, s == 0, s == tps - 1


def _fill_lhs(lhs_ref, hp_ref, h_ref, hn_ref, first, last, tm):
    zero = jnp.zeros((HALO, lhs_ref.shape[1]), lhs_ref.dtype)
    lhs_ref[0:HALO, :] = jnp.where(first, zero, hp_ref[...])
    lhs_ref[HALO:HALO + tm, :] = h_ref[...]
    lhs_ref[HALO + tm:HALO + tm + HALO, :] = jnp.where(last, zero, hn_ref[...])


def _h_specs(tm, d, m):
    prev, nxt = _halo_specs(tm, d, 0, m)
    return [pl.BlockSpec((tm, d), lambda i, *_: (i, 0)), prev, nxt]


def _proj_conv_a_body(h_ref, hp_ref, hn_ref, wcg_ref, whh_ref, wb_ref, cw_ref, o_ref, lhs_ref, *, tm, tps):
    _, first, last = _seq_edges(tps)

    @pl.when(pl.program_id(1) == 0)
    def _():
        _fill_lhs(lhs_ref, hp_ref, h_ref, hn_ref, first, last, tm)

    lhs = lhs_ref[...]
    u = (jnp.dot(lhs, wcg_ref[...], preferred_element_type=F32)
         * jnp.dot(lhs, whh_ref[...], preferred_element_type=F32))
    b = jnp.dot(lhs_ref[HALO:HALO + tm, :], wb_ref[...], preferred_element_type=F32)
    taps = cw_ref.shape[0]
    pad = taps // 2
    conv = cw_ref[0:1, :] * u[HALO - pad:HALO - pad + tm, :]
    for t in range(1, taps):
        conv = conv + cw_ref[t:t + 1, :] * u[HALO - pad + t:HALO - pad + t + tm, :]
    o_ref[...] = (b * conv).astype(o_ref.dtype)


def _proj_conv_a(h, w_z, conv_w, width, seq_len):
    m, d = h.shape
    tm = _pick(seq_len, 1024)
    tc = _pick(width, 512)
    nb = width // tc

    def wcol(group):
        return pl.BlockSpec((d, tc), lambda i, j: (0, group * nb + j))

    return pl.pallas_call(
        functools.partial(_proj_conv_a_body, tm=tm, tps=seq_len // tm),
        grid=(m // tm, nb),
        in_specs=_h_specs(tm, d, m) + [wcol(2), wcol(3), wcol(4),
                                       pl.BlockSpec((conv_w.shape[0], tc), lambda i, j: (0, j))],
        out_specs=pl.BlockSpec((tm, tc), lambda i, j: (i, j)),
        out_shape=jax.ShapeDtypeStruct((m, width), BF16),
        scratch_shapes=[pltpu.VMEM((tm + 2 * HALO, d), BF16)],
        compiler_params=_cparams("arbitrary", "arbitrary"),
        name="proj_conv_a",
    )(h, h, h, w_z, w_z, w_z, conv_w)


def _window_sum(ext, win, tm):
    half = win // 2
    part, length, span = ext, ext.shape[0], 1
    while span < half:
        length -= SUBLANES
        part = part[0:length, :] + part[span:span + length, :]
        span *= 2
    return part[HALO - half:HALO - half + tm, :] + part[HALO:HALO + tm, :]


def _proj_pool_body(h_ref, hp_ref, hn_ref, w_ref, o_ref, lhs_ref, *, tm, tps, seq_len, group):
    s, first, last = _seq_edges(tps)
    _fill_lhs(lhs_ref, hp_ref, h_ref, hn_ref, first, last, tm)
    u = jnp.dot(lhs_ref[...], w_ref[...], preferred_element_type=F32)
    u = jnp.concatenate([u, jnp.zeros((SUBLANES, u.shape[1]), F32)], axis=0)
    pos = s * tm + lax.broadcasted_iota(jnp.int32, (tm, 1), 0)
    for g, win in enumerate(POOL_WINDOWS):
        cols = slice(g * group, (g + 1) * group)
        ext = u[:, cols]
        lo = jnp.maximum(pos - win // 2, 0)
        hi = jnp.minimum(pos - win // 2 + win, seq_len)
        cnt = (hi - lo).astype(F32)
        o_ref[:, cols] = (_window_sum(ext, win, tm) / cnt - ext[HALO:HALO + tm, :]).astype(o_ref.dtype)


def _proj_pool(h, w_z, width, seq_len):
    m, d = h.shape
    assert max(POOL_WINDOWS) // 2 <= SUBLANES and max(POOL_WINDOWS) <= HALO
    tm = _pick(seq_len, 512)
    return pl.pallas_call(
        functools.partial(_proj_pool_body, tm=tm, tps=seq_len // tm, seq_len=seq_len,
                          group=width // len(POOL_WINDOWS)),
        grid=(m // tm,),
        in_specs=_h_specs(tm, d, m) + [pl.BlockSpec((d, width), lambda i: (0, 5))],
        out_specs=pl.BlockSpec((tm, width), lambda i: (i, 0)),
        out_shape=jax.ShapeDtypeStruct((m, width), BF16),
        scratch_shapes=[pltpu.VMEM((tm + 2 * HALO, d), BF16)],
        compiler_params=_cparams("arbitrary"),
        name="proj_pool",
    )(h, h, h, w_z)


def _proj_conv_d_body(h_ref, hp_ref, hn_ref, wa_ref, wgt_ref, w_ref, cb_ref, lg_ref, lb_ref, o_ref,
                      lhs_ref, ext_ref, sh_ref, *, tl, tps):
    _, first, last = _seq_edges(tps)
    _fill_lhs(lhs_ref, hp_ref, h_ref, hn_ref, first, last, tl)
    lhs = lhs_ref[...]
    rows = sh_ref.shape[1]
    taps = w_ref.shape[0]
    pad = taps // 2
    slab = 2 * LANE
    ys = []
    for c0 in range(0, w_ref.shape[1], slab):
        cols = slice(c0, c0 + slab)
        ext_ref[:, cols] = (jnp.dot(lhs, wa_ref[:, cols], preferred_element_type=F32)
                            * _sigmoid(jnp.dot(lhs, wgt_ref[:, cols], preferred_element_type=F32)))
        for r in range(SUBLANES - 1):
            sh_ref[r, :, cols] = ext_ref[r + 1:r + 1 + rows, cols]
        acc = None
        for k in range(taps):
            off = HALO - pad + k
            if off % SUBLANES == 0:
                tap = ext_ref[off:off + tl, cols]
            else:
                base = off - off % SUBLANES
                tap = sh_ref[off % SUBLANES - 1, base:base + tl, cols]
            term = w_ref[k:k + 1, cols] * tap
            acc = term if acc is None else acc + term
        ys.append(acc)
    y = jnp.concatenate(ys, axis=1) + cb_ref[...]
    mu = jnp.mean(y, axis=-1, keepdims=True)
    yc = y - mu
    var = jnp.mean(yc * yc, axis=-1, keepdims=True)
    yn = yc * lax.rsqrt(var + LN_EPS) * lg_ref[...] + lb_ref[...]
    o_ref[...] = (yn * _sigmoid(yn)).astype(o_ref.dtype)


def _proj_conv_d(h, w_z, conv_w, conv_b, ln_g, ln_b, width, seq_len):
    m, d = h.shape
    tl = _pick(seq_len, 256)
    vec = pl.BlockSpec((1, width), lambda i: (0, 0))
    return pl.pallas_call(
        functools.partial(_proj_conv_d_body, tl=tl, tps=seq_len // tl),
        grid=(m // tl,),
        in_specs=_h_specs(tl, d, m) + [
            pl.BlockSpec((d, width), lambda i: (0, 0)),
            pl.BlockSpec((d, width), lambda i: (0, 1)),
            pl.BlockSpec(conv_w.shape, lambda i: (0, 0)),
            vec, vec, vec,
        ],
        out_specs=pl.BlockSpec((tl, width), lambda i: (i, 0)),
        out_shape=jax.ShapeDtypeStruct((m, width), BF16),
        scratch_shapes=[pltpu.VMEM((tl + 2 * HALO, d), BF16),
                        pltpu.VMEM((tl + 2 * HALO, width), F32),
                        pltpu.VMEM((SUBLANES - 1, tl + 2 * HALO - SUBLANES, width), F32)],
        compiler_params=_cparams("arbitrary"),
        name="proj_conv_d",
    )(h, h, h, w_z, w_z, conv_w, conv_b.reshape(1, width), ln_g.reshape(1, width), ln_b.reshape(1, width))


def _ffn_up_body(x_ref, xp_ref, xn_ref, g_ref, sh_ref, sc_ref, wg_ref, wv_ref, cw_ref, cb_ref, f_ref, h_ref,
                 *, tm, tps):
    _, first, last = _seq_edges(tps)

    @pl.when(pl.program_id(1) == 0)
    def _():
        def norm(x):
            ms = jnp.mean(x * x, axis=-1, keepdims=True)
            y = x * lax.rsqrt(ms + EPS) * g_ref[...]
            return (y * (1.0 + sc_ref[...]) + sh_ref[...]).astype(h_ref.dtype)

        zero = jnp.zeros((HALO, h_ref.shape[1]), h_ref.dtype)
        h_ref[0:HALO, :] = jnp.where(first, zero, norm(xp_ref[...]))
        h_ref[HALO:HALO + tm, :] = norm(x_ref[...])
        h_ref[HALO + tm:HALO + tm + HALO, :] = jnp.where(last, zero, norm(xn_ref[...]))

    gate = jnp.dot(h_ref[...], wg_ref[...], preferred_element_type=F32)
    val = jnp.dot(h_ref[HALO:HALO + tm, :], wv_ref[...], preferred_element_type=F32)
    taps = cw_ref.shape[0]
    pad = taps // 2
    conv = cw_ref[0:1, :] * gate[HALO - pad:HALO - pad + tm, :]
    for t in range(1, taps):
        conv = conv + cw_ref[t:t + 1, :] * gate[HALO - pad + t:HALO - pad + t + tm, :]
    g = conv + cb_ref[...]
    f_ref[...] = (g * _sigmoid(g) * val).astype(f_ref.dtype)


def _ffn_up(x, gain, mods, k_shift, rows_per_mod, seq_len, w_up, conv_w, conv_b):
    m, d = x.shape
    d_ff = conv_w.shape[1]
    tc = _pick(d_ff, 512)
    nj = d_ff // tc
    tm = _pick(seq_len, 1024)
    per = tm // HALO
    last_blk = m // HALO - 1
    return pl.pallas_call(
        functools.partial(_ffn_up_body, tm=tm, tps=seq_len // tm),
        grid=(m // tm, nj),
        in_specs=[
            pl.BlockSpec((tm, d), lambda i, j: (i, 0)),
            pl.BlockSpec((HALO, d), lambda i, j: (jnp.maximum(i * per - 1, 0), 0)),
            pl.BlockSpec((HALO, d), lambda i, j: (jnp.minimum((i + 1) * per, last_blk), 0)),
            pl.BlockSpec((1, d), lambda i, j: (0, 0)),
            pl.BlockSpec((None, 1, d), lambda i, j: (i * tm // rows_per_mod, 0, k_shift)),
            pl.BlockSpec((None, 1, d), lambda i, j: (i * tm // rows_per_mod, 0, k_shift + 1)),
            pl.BlockSpec((d, tc), lambda i, j: (0, j)),
            pl.BlockSpec((d, tc), lambda i, j: (0, nj + j)),
            pl.BlockSpec((conv_w.shape[0], tc), lambda i, j: (0, j)),
            pl.BlockSpec((1, tc), lambda i, j: (0, j)),
        ],
        out_specs=pl.BlockSpec((tm, tc), lambda i, j: (i, j)),
        out_shape=jax.ShapeDtypeStruct((m, d_ff), BF16),
        scratch_shapes=[pltpu.VMEM((tm + 2 * HALO, d), BF16)],
        compiler_params=_cparams("arbitrary", "arbitrary"),
        name="ffn_up",
    )(x, x, x, gain.reshape(1, d), mods, mods, w_up, w_up, conv_w, conv_b.reshape(1, d_ff))


def _rope_vreg(bn, cos_ref, sin_ref):
    return bn * cos_ref[...] + pltpu.roll(bn, LANE // 2, 1) * sin_ref[...]


def _norm_qkv_body(x_ref, g_ref, sh_ref, sc_ref, wz_ref, qg_ref, wq_ref, kvg_ref, wkv_ref, qhg_ref, khg_ref,
                   cos_ref, sin_ref, h_ref, q_ref, k_ref, v_ref, *, q_lora, kv_lora, heads, qk_dim, scale):
    x = x_ref[...]
    y = x * lax.rsqrt(jnp.mean(x * x, axis=-1, keepdims=True) + EPS) * g_ref[...]
    h = (y * (1.0 + sc_ref[...]) + sh_ref[...]).astype(h_ref.dtype)
    h_ref[...] = h
    z = jnp.dot(h, wz_ref[...], preferred_element_type=F32)
    lane = lax.broadcasted_iota(jnp.int32, (1, LANE), 1)
    rope_lanes = lane < (LANE // 2)
    slot = 2 * LANE

    def rms(v, g):
        return v * lax.rsqrt(jnp.mean(v * v, axis=-1, keepdims=True) + EPS) * g

    qn = rms(z[:, :q_lora], qg_ref[...]).astype(BF16)
    yq = jnp.dot(qn, wq_ref[...], preferred_element_type=F32)
    g_nope, g_rope = qhg_ref[:, :LANE], qhg_ref[:, LANE:]
    for h in range(heads):
        a = yq[:, h * slot:h * slot + LANE]
        b = yq[:, h * slot + LANE:(h + 1) * slot]
        ss = jnp.sum(a * a + jnp.where(rope_lanes, b * b, 0.0), axis=-1, keepdims=True)
        r = lax.rsqrt(ss / qk_dim + EPS) * scale
        q_ref[:, h * slot:h * slot + LANE] = (a * r * g_nope).astype(q_ref.dtype)
        q_ref[:, h * slot + LANE:(h + 1) * slot] = _rope_vreg(b * r * g_rope, cos_ref, sin_ref).astype(q_ref.dtype)

    kvn = rms(z[:, q_lora:q_lora + kv_lora], kvg_ref[...]).astype(BF16)
    ykv = jnp.dot(kvn, wkv_ref[...], preferred_element_type=F32)
    kr = z[:, q_lora + kv_lora:q_lora + kv_lora + LANE]
    kr_ss = jnp.sum(jnp.where(rope_lanes, kr * kr, 0.0), axis=-1, keepdims=True)
    g_nope, g_rope = khg_ref[:, :LANE], khg_ref[:, LANE:]
    for h in range(heads):
        a = ykv[:, h * LANE:(h + 1) * LANE]
        ss = jnp.sum(a * a, axis=-1, keepdims=True) + kr_ss
        r = lax.rsqrt(ss / qk_dim + EPS)
        k_ref[:, h * slot:h * slot + LANE] = (a * r * g_nope).astype(k_ref.dtype)
        k_ref[:, h * slot + LANE:(h + 1) * slot] = _rope_vreg(kr * r * g_rope, cos_ref, sin_ref).astype(k_ref.dtype)
    ones = jnp.ones((z.shape[0], LANE), v_ref.dtype)
    for h in range(heads):
        v_ref[:, h * slot:h * slot + LANE] = ykv[:, (heads + h) * LANE:(heads + h + 1) * LANE].astype(v_ref.dtype)
        v_ref[:, h * slot + LANE:(h + 1) * slot] = ones


def _norm_qkv(x, gain, mods, k_shift, rows_per_mod, lw, cos_t, sin_t, seq_len, dims, use_rope):
    m, d = x.shape
    heads, q_lora, kv_lora, qk_dim = dims["heads"], dims["q_lora"], dims["kv_lora"], dims["qk_dim"]
    w_zq = lw["w_z"][:, -dims["nq"]:]
    tm = _pick(seq_len, 512)
    tps = seq_len // tm
    if use_rope:
        tab = pl.BlockSpec((tm, LANE), lambda i: (i % tps, 0))
    else:
        tab = pl.BlockSpec((tm, LANE), lambda i: (0, 0))

    def full(a):
        return pl.BlockSpec(a.shape, lambda i: (0, 0))

    qw = heads * 2 * LANE
    row = pl.BlockSpec((tm, qw), lambda i: (i, 0))
    return pl.pallas_call(
        functools.partial(_norm_qkv_body, q_lora=q_lora, kv_lora=kv_lora, heads=heads, qk_dim=qk_dim,
                          scale=qk_dim ** -0.5 * LOG2E),
        grid=(m // tm,),
        in_specs=[
            pl.BlockSpec((tm, d), lambda i: (i, 0)),
            pl.BlockSpec((1, d), lambda i: (0, 0)),
            pl.BlockSpec((None, 1, d), lambda i: (i * tm // rows_per_mod, 0, k_shift)),
            pl.BlockSpec((None, 1, d), lambda i: (i * tm // rows_per_mod, 0, k_shift + 1)),
            full(w_zq),
            full(lw["q_norm_g"]), full(lw["wq"]), full(lw["kv_norm_g"]), full(lw["wkv"]),
            full(lw["q_head_g"]), full(lw["k_head_g"]), tab, tab,
        ],
        out_specs=[pl.BlockSpec((tm, d), lambda i: (i, 0)), row, row, row],
        out_shape=[jax.ShapeDtypeStruct((m, d), BF16)] + [jax.ShapeDtypeStruct((m, qw), BF16)] * 3,
        compiler_params=_cparams("arbitrary"),
        name="norm_qkv",
    )(x, gain.reshape(1, d), mods, mods, w_zq, lw["q_norm_g"], lw["wq"], lw["kv_norm_g"], lw["wkv"],
      lw["q_head_g"], lw["k_head_g"], cos_t, sin_t)


def _attn_update(q, k, v1, carry):
    m_i, acc = carry
    s = lax.dot_general(q, k, (((1,), (1,)), ((), ())), preferred_element_type=F32)
    m_new = jnp.maximum(m_i, jnp.max(s, axis=-1, keepdims=True))
    alpha = jnp.exp2(m_i - m_new)
    p = jnp.exp2(s - m_new)
    acc = alpha * acc + jnp.dot(p.astype(v1.dtype), v1, preferred_element_type=F32)
    return m_new, acc


def _attn_body(*refs, tk, n_lat):
    if n_lat:
        q_ref, kc_ref, vc_ref, kx_ref, vx_ref, o_ref = refs
    else:
        q_ref, kc_ref, vc_ref, o_ref = refs
    q = q_ref[...]
    tq = q.shape[0]
    vd = o_ref.shape[-1]
    carry = (jnp.full((tq, 1), -jnp.inf, F32), jnp.zeros((tq, 2 * vd), F32))
    carry = _attn_update(q, kc_ref[...], vc_ref[...], carry)
    for c in range(n_lat):
        carry = _attn_update(q, kx_ref[c * tk:(c + 1) * tk, :], vx_ref[c * tk:(c + 1) * tk, :], carry)
    _, acc = carry
    o_ref[...] = (acc[:, :vd] / acc[:, vd:]).astype(o_ref.dtype)


def _attention(q, kc, vc, kx, vx, batch, heads):
    lq = q.shape[0] // batch
    lc = kc.shape[0] // batch
    slot, vd = 2 * LANE, LANE
    tq = _pick(lq, 2048)
    q3 = q.reshape(batch, lq, heads * slot)
    args = [q3, kc.reshape(batch, lc, heads * slot), vc.reshape(batch, lc, heads * slot)]
    in_specs = [
        pl.BlockSpec((None, tq, slot), lambda b, h, i: (b, i, h)),
        pl.BlockSpec((None, lc, slot), lambda b, h, i: (b, 0, h)),
        pl.BlockSpec((None, lc, slot), lambda b, h, i: (b, 0, h)),
    ]
    n_lat, tk = 0, 0
    if kx is not None:
        lx = kx.shape[0] // batch
        tk = _pick(lx, 1024)
        n_lat = lx // tk
        args += [kx.reshape(batch, lx, heads * slot), vx.reshape(batch, lx, heads * slot)]
        in_specs += [
            pl.BlockSpec((None, lx, slot), lambda b, h, i: (b, 0, h)),
            pl.BlockSpec((None, lx, slot), lambda b, h, i: (b, 0, h)),
        ]
    out = pl.pallas_call(
        functools.partial(_attn_body, tk=tk, n_lat=n_lat),
        grid=(batch, heads, lq // tq),
        in_specs=in_specs,
        out_specs=pl.BlockSpec((None, tq, vd), lambda b, h, i: (b, i, h)),
        out_shape=jax.ShapeDtypeStruct((batch, lq, heads * vd), BF16),
        compiler_params=_cparams("arbitrary", "arbitrary", "arbitrary"),
        name="attention_x" if n_lat else "attention_c",
    )(*args)
    return out.reshape(batch * lq, heads * vd)


def _merge_body(h_ref, a_ref, t_ref, p_ref, d_ref, wga_ref, wgt_ref, wgp_ref, wgd_ref, wa_ref, wt_ref, wp_ref,
                ps_ref, wd_ref, o_ref):
    h = h_ref[...]

    def gate(wg_ref):
        return _sigmoid(jnp.dot(h, wg_ref[...], preferred_element_type=F32))

    def proj(x_ref, w):
        return jnp.dot(x_ref[...], w, preferred_element_type=F32)

    acc = gate(wga_ref) * proj(a_ref, wa_ref[...])
    acc = acc + gate(wgt_ref) * proj(t_ref, wt_ref[...])
    acc = acc + gate(wgp_ref) * (proj(p_ref, wp_ref[...]) * ps_ref[...])
    acc = acc + gate(wgd_ref) * proj(d_ref, wd_ref[...])
    o_ref[...] = acc.astype(o_ref.dtype)


def _merge(h, act_a, att, act_p, act_d, lw, rows_per_seq):
    m, d = h.shape
    n_branch = lw["wg"].shape[1] // d
    assert n_branch == 4 and len(POOL_WINDOWS) == n_branch
    tn = d // n_branch
    group = act_p.shape[1] // n_branch
    tm = _pick(rows_per_seq, 512)

    def act(a):
        return pl.BlockSpec((tm, a.shape[1]), lambda j, i: (i, 0))

    def wcol(w):
        return pl.BlockSpec((w.shape[0], tn), lambda j, i: (0, j))

    return pl.pallas_call(
        _merge_body,
        grid=(n_branch, m // tm),
        in_specs=[
            act(h), act(act_a), act(att),
            pl.BlockSpec((tm, group), lambda j, i: (i, j)),
            act(act_d),
            *[pl.BlockSpec((d, tn), lambda j, i, b=b: (0, b * n_branch + j)) for b in range(n_branch)],
            wcol(lw["w_a_out"]), wcol(lw["w_mla_out"]),
            pl.BlockSpec((None, group, tn), lambda j, i: (j, 0, 0)),
            pl.BlockSpec((1, tn), lambda j, i: (0, j)),
            wcol(lw["w_d_out"]),
        ],
        out_specs=pl.BlockSpec((tm, tn), lambda j, i: (i, j)),
        out_shape=jax.ShapeDtypeStruct((m, d), BF16),
        compiler_params=_cparams("arbitrary", "arbitrary"),
        name="merge",
    )(h, act_a, att, act_p, act_d, lw["wg"], lw["wg"], lw["wg"], lw["wg"], lw["w_a_out"], lw["w_mla_out"],
      lw["w_pool"], lw["pool_scale"], lw["w_d_out"])


def _rope_perm(rope_dim):
    nf = rope_dim // 4
    j = jnp.arange(rope_dim)
    return jnp.where((j % (2 * nf)) < nf, j + nf, j - nf)


def _rope_tables(seq_len, rope_dim):
    rows = seq_len // GRID_W
    row = jnp.broadcast_to(jnp.arange(rows)[:, None], (rows, GRID_W)).reshape(seq_len)
    col = jnp.broadcast_to(jnp.arange(GRID_W)[None, :], (rows, GRID_W)).reshape(seq_len)
    nf = rope_dim // 4
    inv = ROPE_THETA ** (-jnp.arange(nf, dtype=F32) / nf)
    ang = jnp.stack([row, col], axis=-1).astype(F32)[:, :, None] * inv
    cos, sin = jnp.cos(ang), jnp.sin(ang)
    cos_t = jnp.concatenate([cos, cos], axis=-1).reshape(seq_len, rope_dim)
    sin_t = jnp.concatenate([-sin, sin], axis=-1).reshape(seq_len, rope_dim)
    pad = jnp.zeros((seq_len, LANE - rope_dim), F32)
    return jnp.concatenate([cos_t, pad], axis=-1), jnp.concatenate([sin_t, pad], axis=-1)


def _layer_weights(l, p, dims):
    d, wa, wc, wd = dims["d"], dims["wa"], dims["wc"], dims["wd"]
    heads, nope, rope, vdim = dims["heads"], dims["nope"], dims["rope"], dims["vdim"]
    q_lora, kv_lora, nq = dims["q_lora"], dims["kv_lora"], dims["nq"]
    perm = _rope_perm(rope)
    w_in = p["w_in"][l]
    off_q = 3 * wa
    off_kv = off_q + q_lora
    off_kr = off_kv + kv_lora
    off_p = off_kr + rope
    off_d = off_p + wc
    off_g = off_d + 2 * wd
    kr = w_in[:, off_kr:off_p]
    qkv_used = q_lora + kv_lora + 2 * rope
    w_z = jnp.concatenate([
        w_in[:, off_d:off_g],
        w_in[:, wa:3 * wa],
        w_in[:, :wa],
        w_in[:, off_p:off_d],
        w_in[:, off_q:off_kr],
        kr, kr[:, perm],
        jnp.zeros((d, nq - qkv_used), w_in.dtype),
    ], axis=1).astype(BF16)
    wg = w_in[:, off_g:].astype(BF16)

    qk = nope + rope
    wq = p["w_q_up"][l].reshape(q_lora, heads, qk)
    wq = jnp.concatenate([wq, wq[:, :, nope:][:, :, perm]], axis=-1).reshape(q_lora, heads * 2 * LANE)
    wkv = p["w_kv_up"][l].reshape(kv_lora, heads, nope + vdim)
    wkv = jnp.concatenate([wkv[:, :, :nope].reshape(kv_lora, heads * nope),
                           wkv[:, :, nope:].reshape(kv_lora, heads * vdim)], axis=1)

    def head_gain(g):
        return jnp.concatenate([g, g[nope:][perm]]).reshape(1, 2 * LANE)

    return {
        "w_z": w_z, "wg": wg,
        "wq": wq.astype(BF16), "wkv": wkv.astype(BF16),
        "q_norm_g": p["q_norm_g"][l].reshape(1, q_lora), "kv_norm_g": p["kv_norm_g"][l].reshape(1, kv_lora),
        "q_head_g": head_gain(p["q_head_g"][l]), "k_head_g": head_gain(p["k_head_g"][l]),
        "w_a_out": p["w_a_out"][l].astype(BF16), "w_mla_out": p["w_mla_out"][l].astype(BF16),
        "w_pool": p["w_pool"][l].astype(BF16), "pool_scale": p["pool_scale"][l].reshape(1, d),
        "w_d_out": p["w_d_out"][l].astype(BF16), "w_out": p["w_out"][l].astype(BF16),
        "w_up": p["w_up"][l].astype(BF16), "w_down": p["w_down"][l].astype(BF16),
    }


def _token_mixer_inputs(xs, mods, rows_per_mod, seq_len, l, p, lw, dims, tables):
    return _norm_qkv(xs, p["norm1_g"][l], mods, 0, rows_per_mod, lw, tables[0], tables[1], seq_len, dims, tables[2])


def _finish_layer(xs, mods, rows_per_mod, seq_len, l, p, lw, dims, h, att):
    wa, wc, wd = dims["wa"], dims["wc"], dims["wd"]
    act_a = _proj_conv_a(h, lw["w_z"], p["conv_a_w"][l], wa, seq_len)
    act_p = _proj_pool(h, lw["w_z"], wc, seq_len)
    act_d = _proj_conv_d(h, lw["w_z"], p["conv_d_w"][l], p["conv_d_b"][l], p["cd_ln_g"][l], p["cd_ln_b"][l],
                         wd, seq_len)
    merged = _merge(h, act_a, att, act_p, act_d, lw, seq_len)
    x1 = _mm_res(merged, lw["w_out"], xs, mods, 2, rows_per_mod, tm=512, tn=2048)
    f = _ffn_up(x1, p["norm2_g"][l], mods, 3, rows_per_mod, seq_len, lw["w_up"], p["conv_ff_w"][l],
                p["conv_ff_b"][l])
    return _mm_res(f, lw["w_down"], x1, mods, 5, rows_per_mod, tm=1024, tn=512)


def kernel(x, c, ctx, c_ctx, ada_w, ada_b, norm1_g, w_in, conv_a_w, w_a_out, q_norm_g, w_q_up, kv_norm_g, w_kv_up, q_head_g, k_head_g, w_mla_out, w_pool, pool_scale, conv_d_w, conv_d_b, cd_ln_g, cd_ln_b, w_d_out, w_out, norm2_g, w_up, conv_ff_w, conv_ff_b, w_down):
    p = dict(norm1_g=norm1_g, w_in=w_in, conv_a_w=conv_a_w, w_a_out=w_a_out, q_norm_g=q_norm_g, w_q_up=w_q_up,
             kv_norm_g=kv_norm_g, w_kv_up=w_kv_up, q_head_g=q_head_g, k_head_g=k_head_g, w_mla_out=w_mla_out,
             w_pool=w_pool, pool_scale=pool_scale, conv_d_w=conv_d_w, conv_d_b=conv_d_b, cd_ln_g=cd_ln_g,
             cd_ln_b=cd_ln_b, w_d_out=w_d_out, w_out=w_out, norm2_g=norm2_g, w_up=w_up, conv_ff_w=conv_ff_w,
             conv_ff_b=conv_ff_b, w_down=w_down)
    batch, seq, d = x.shape
    n_ctx = ctx.shape[1]
    depth = ada_w.shape[0]
    qk_dim = q_head_g.shape[1]
    heads = w_q_up.shape[2] // qk_dim
    vdim = w_mla_out.shape[1] // heads
    nope = w_kv_up.shape[2] // heads - vdim
    rope = qk_dim - nope
    wa, wc, wd = conv_a_w.shape[2], w_pool.shape[1] * w_pool.shape[2], conv_d_w.shape[2]
    q_lora, kv_lora = w_q_up.shape[1], w_kv_up.shape[1]
    assert nope == LANE and vdim == LANE and 2 * rope == LANE, "head layout assumes 128 | 64 | 128 dims"
    assert wa == wc == wd and wa % (len(POOL_WINDOWS) * LANE) == 0
    nq = q_lora + kv_lora + 2 * rope
    while (6 * wa) % nq or nq % LANE:
        nq += LANE
    dims = dict(d=d, wa=wa, wc=wc, wd=wd, heads=heads, nope=nope, rope=rope, vdim=vdim, qk_dim=qk_dim,
                q_lora=q_lora, kv_lora=kv_lora, nq=nq)

    rows = -(-(batch + 1) // 8) * 8
    cc = jnp.zeros((rows, d), F32).at[:batch].set(c).at[batch].set(c_ctx)
    mods = _ada(cc, ada_w, ada_b)

    cos_t, sin_t = _rope_tables(seq, rope)
    ones_t = jnp.concatenate([jnp.ones((n_ctx, rope), F32), jnp.zeros((n_ctx, LANE - rope), F32)], axis=-1)
    zeros_t = jnp.zeros((n_ctx, LANE), F32)
    tab_x = (cos_t, sin_t, True)
    tab_c = (ones_t, zeros_t, False)

    xs = x.reshape(batch * seq, d)
    cs = ctx.reshape(batch * n_ctx, d)
    for l in range(depth):
        last = l == depth - 1
        lw = _layer_weights(l, p, dims)
        mods_x = mods[l, :batch].reshape(batch, 1, -1)
        mods_c = mods[l, batch:batch + 1].reshape(1, 1, -1)
        hc, qc, kc, vc = _token_mixer_inputs(cs, mods_c, batch * n_ctx, n_ctx, l, p, lw, dims, tab_c)
        hx, qx, kx, vx = _token_mixer_inputs(xs, mods_x, seq, seq, l, p, lw, dims, tab_x)
        att_x = _attention(qx, kc, vc, kx, vx, batch, heads)
        xs_new = _finish_layer(xs, mods_x, seq, seq, l, p, lw, dims, hx, att_x)
        if not last:
            att_c = _attention(qc, kc, vc, None, None, batch, heads)
            cs = _finish_layer(cs, mods_c, batch * n_ctx, n_ctx, l, p, lw, dims, hc, att_c)
        xs = xs_new
    return xs.reshape(batch, seq, d)
```

```python
import functools

import jax
import jax.numpy as jnp
from jax import lax
from jax.experimental import pallas as pl
from jax.experimental.pallas import tpu as pltpu

GRID_W = 64
ROPE_THETA = 10000.0
POOL_WINDOWS = (2, 4, 8, 16)
EPS = 1e-6
LN_EPS = 1e-5

LOG2E = 1.4426950408889634
LANE = 128
SUBLANES = 8
HALO = 16
VMEM_LIMIT = 56 * 1024 * 1024

F32 = jnp.float32
BF16 = jnp.bfloat16


def _cparams(*sem):
    return pltpu.CompilerParams(dimension_semantics=sem, vmem_limit_bytes=VMEM_LIMIT)


def _sigmoid(x):
    return 0.5 * jnp.tanh(0.5 * x) + 0.5


def _pick(n, pref):
    if n <= pref:
        return n
    t = pref
    while n % t:
        t //= 2
    return t


def _ada_body(cc_ref, w_ref, b_ref, o_ref):
    cc = cc_ref[...]
    s = (cc * jax.nn.sigmoid(cc)).astype(BF16)
    o_ref[...] = jnp.dot(s, w_ref[...].astype(BF16), preferred_element_type=F32) + b_ref[...]


def _ada(cc, ada_w, ada_b):
    depth, d, n = ada_w.shape
    rows = cc.shape[0]
    tn = _pick(n, 1024)
    return pl.pallas_call(
        _ada_body,
        grid=(depth, n // tn),
        in_specs=[
            pl.BlockSpec((rows, d), lambda l, j: (0, 0)),
            pl.BlockSpec((None, d, tn), lambda l, j: (l, 0, j)),
            pl.BlockSpec((None, 1, tn), lambda l, j: (l, 0, j)),
        ],
        out_specs=pl.BlockSpec((None, rows, tn), lambda l, j: (l, 0, j)),
        out_shape=jax.ShapeDtypeStruct((depth, rows, n), F32),
        compiler_params=_cparams("arbitrary", "arbitrary"),
        name="ada",
    )(cc, ada_w, ada_b.reshape(depth, 1, n))


def _mm_res_body(a_ref, w_ref, r_ref, gate_ref, o_ref):
    acc = jnp.dot(a_ref[...], w_ref[...], preferred_element_type=F32)
    o_ref[...] = r_ref[...] + gate_ref[...] * acc


def _mm_res(a, w, res, mods, k_gate, rows_per_mod, *, tm, tn):
    m, k = a.shape
    n = w.shape[1]
    tm = _pick(rows_per_mod, tm)
    tn = _pick(n, tn)
    nj = n // tn
    return pl.pallas_call(
        _mm_res_body,
        grid=(m // tm, nj),
        in_specs=[
            pl.BlockSpec((tm, k), lambda i, j: (i, 0)),
            pl.BlockSpec((k, tn), lambda i, j: (0, j)),
            pl.BlockSpec((tm, tn), lambda i, j: (i, j)),
            pl.BlockSpec((None, 1, tn), lambda i, j: (i * tm // rows_per_mod, 0, k_gate * nj + j)),
        ],
        out_specs=pl.BlockSpec((tm, tn), lambda i, j: (i, j)),
        out_shape=jax.ShapeDtypeStruct((m, n), F32),
        compiler_params=_cparams("arbitrary", "arbitrary"),
        name="mm_res",
    )(a, w, res, mods)


def _halo_specs(tl, width, col_block, n_rows):
    per = tl // HALO
    last = n_rows // HALO - 1
    prev = pl.BlockSpec((HALO, width), lambda i, *_: (jnp.maximum(i * per - 1, 0), col_block))
    nxt = pl.BlockSpec((HALO, width), lambda i, *_: (jnp.minimum((i + 1) * per, last), col_block))
    return prev, nxt


def _seq_edges(tps):
    s = pl.program_id(0) % tps
    return s, s == 0, s == tps - 1


def _fill_lhs(lhs_ref, hp_ref, h_ref, hn_ref, first, last, tm):
    zero = jnp.zeros((HALO, lhs_ref.shape[1]), lhs_ref.dtype)
    lhs_ref[0:HALO, :] = jnp.where(first, zero, hp_ref[...])
    lhs_ref[HALO:HALO + tm, :] = h_ref[...]
    lhs_ref[HALO + tm:HALO + tm + HALO, :] = jnp.where(last, zero, hn_ref[...])


def _h_specs(tm, d, m):
    prev, nxt = _halo_specs(tm, d, 0, m)
    return [pl.BlockSpec((tm, d), lambda i, *_: (i, 0)), prev, nxt]


def _proj_conv_a_body(h_ref, hp_ref, hn_ref, wcg_ref, whh_ref, wb_ref, cw_ref, o_ref, lhs_ref, *, tm, tps):
    _, first, last = _seq_edges(tps)

    @pl.when(pl.program_id(1) == 0)
    def _():
        _fill_lhs(lhs_ref, hp_ref, h_ref, hn_ref, first, last, tm)

    lhs = lhs_ref[...]
    u = (jnp.dot(lhs, wcg_ref[...], preferred_element_type=F32)
         * jnp.dot(lhs, whh_ref[...], preferred_element_type=F32))
    b = jnp.dot(lhs_ref[HALO:HALO + tm, :], wb_ref[...], preferred_element_type=F32)
    taps = cw_ref.shape[0]
    pad = taps // 2
    conv = cw_ref[0:1, :] * u[HALO - pad:HALO - pad + tm, :]
    for t in range(1, taps):
        conv = conv + cw_ref[t:t + 1, :] * u[HALO - pad + t:HALO - pad + t + tm, :]
    o_ref[...] = (b * conv).astype(o_ref.dtype)


def _proj_conv_a(h, w_z, conv_w, width, seq_len):
    m, d = h.shape
    tm = _pick(seq_len, 1024)
    tc = _pick(width, 512)
    nb = width // tc

    def wcol(group):
        return pl.BlockSpec((d, tc), lambda i, j: (0, group * nb + j))

    return pl.pallas_call(
        functools.partial(_proj_conv_a_body, tm=tm, tps=seq_len // tm),
        grid=(m // tm, nb),
        in_specs=_h_specs(tm, d, m) + [wcol(2), wcol(3), wcol(4),
                                       pl.BlockSpec((conv_w.shape[0], tc), lambda i, j: (0, j))],
        out_specs=pl.BlockSpec((tm, tc), lambda i, j: (i, j)),
        out_shape=jax.ShapeDtypeStruct((m, width), BF16),
        scratch_shapes=[pltpu.VMEM((tm + 2 * HALO, d), BF16)],
        compiler_params=_cparams("arbitrary", "arbitrary"),
        name="proj_conv_a",
    )(h, h, h, w_z, w_z, w_z, conv_w)


def _window_sum(ext, win, tm):
    half = win // 2
    part, length, span = ext, ext.shape[0], 1
    while span < half:
        length -= SUBLANES
        part = part[0:length, :] + part[span:span + length, :]
        span *= 2
    return part[HALO - half:HALO - half + tm, :] + part[HALO:HALO + tm, :]


def _proj_pool_body(h_ref, hp_ref, hn_ref, w_ref, o_ref, lhs_ref, *, tm, tps, seq_len, group):
    s, first, last = _seq_edges(tps)
    _fill_lhs(lhs_ref, hp_ref, h_ref, hn_ref, first, last, tm)
    u = jnp.dot(lhs_ref[...], w_ref[...], preferred_element_type=F32)
    u = jnp.concatenate([u, jnp.zeros((SUBLANES, u.shape[1]), F32)], axis=0)
    pos = s * tm + lax.broadcasted_iota(jnp.int32, (tm, 1), 0)
    for g, win in enumerate(POOL_WINDOWS):
        cols = slice(g * group, (g + 1) * group)
        ext = u[:, cols]
        lo = jnp.maximum(pos - win // 2, 0)
        hi = jnp.minimum(pos - win // 2 + win, seq_len)
        cnt = (hi - lo).astype(F32)
        o_ref[:, cols] = (_window_sum(ext, win, tm) / cnt - ext[HALO:HALO + tm, :]).astype(o_ref.dtype)


def _proj_pool(h, w_z, width, seq_len):
    m, d = h.shape
    assert max(POOL_WINDOWS) // 2 <= SUBLANES and max(POOL_WINDOWS) <= HALO
    tm = _pick(seq_len, 512)
    return pl.pallas_call(
        functools.partial(_proj_pool_body, tm=tm, tps=seq_len // tm, seq_len=seq_len,
                          group=width // len(POOL_WINDOWS)),
        grid=(m // tm,),
        in_specs=_h_specs(tm, d, m) + [pl.BlockSpec((d, width), lambda i: (0, 5))],
        out_specs=pl.BlockSpec((tm, width), lambda i: (i, 0)),
        out_shape=jax.ShapeDtypeStruct((m, width), BF16),
        scratch_shapes=[pltpu.VMEM((tm + 2 * HALO, d), BF16)],
        compiler_params=_cparams("arbitrary"),
        name="proj_pool",
    )(h, h, h, w_z)


def _proj_conv_d_body(h_ref, hp_ref, hn_ref, wa_ref, wgt_ref, w_ref, cb_ref, lg_ref, lb_ref, o_ref,
                      lhs_ref, ext_ref, sh_ref, *, tl, tps):
    _, first, last = _seq_edges(tps)
    _fill_lhs(lhs_ref, hp_ref, h_ref, hn_ref, first, last, tl)
    lhs = lhs_ref[...]
    rows = sh_ref.shape[1]
    taps = w_ref.shape[0]
    pad = taps // 2
    slab = 2 * LANE
    ys = []
    for c0 in range(0, w_ref.shape[1], slab):
        cols = slice(c0, c0 + slab)
        ext_ref[:, cols] = (jnp.dot(lhs, wa_ref[:, cols], preferred_element_type=F32)
                            * _sigmoid(jnp.dot(lhs, wgt_ref[:, cols], preferred_element_type=F32)))
        for r in range(SUBLANES - 1):
            sh_ref[r, :, cols] = ext_ref[r + 1:r + 1 + rows, cols]
        acc = None
        for k in range(taps):
            off = HALO - pad + k
            if off % SUBLANES == 0:
                tap = ext_ref[off:off + tl, cols]
            else:
                base = off - off % SUBLANES
                tap = sh_ref[off % SUBLANES - 1, base:base + tl, cols]
            term = w_ref[k:k + 1, cols] * tap
            acc = term if acc is None else acc + term
        ys.append(acc)
    y = jnp.concatenate(ys, axis=1) + cb_ref[...]
    mu = jnp.mean(y, axis=-1, keepdims=True)
    yc = y - mu
    var = jnp.mean(yc * yc, axis=-1, keepdims=True)
    yn = yc * lax.rsqrt(var + LN_EPS) * lg_ref[...] + lb_ref[...]
    o_ref[...] = (yn * _sigmoid(yn)).astype(o_ref.dtype)


def _proj_conv_d(h, w_z, conv_w, conv_b, ln_g, ln_b, width, seq_len):
    m, d = h.shape
    tl = _pick(seq_len, 256)
    vec = pl.BlockSpec((1, width), lambda i: (0, 0))
    return pl.pallas_call(
        functools.partial(_proj_conv_d_body, tl=tl, tps=seq_len // tl),
        grid=(m // tl,),
        in_specs=_h_specs(tl, d, m) + [
            pl.BlockSpec((d, width), lambda i: (0, 0)),
            pl.BlockSpec((d, width), lambda i: (0, 1)),
            pl.BlockSpec(conv_w.shape, lambda i: (0, 0)),
            vec, vec, vec,
        ],
        out_specs=pl.BlockSpec((tl, width), lambda i: (i, 0)),
        out_shape=jax.ShapeDtypeStruct((m, width), BF16),
        scratch_shapes=[pltpu.VMEM((tl + 2 * HALO, d), BF16),
                        pltpu.VMEM((tl + 2 * HALO, width), F32),
                        pltpu.VMEM((SUBLANES - 1, tl + 2 * HALO - SUBLANES, width), F32)],
        compiler_params=_cparams("arbitrary"),
        name="proj_conv_d",
    )(h, h, h, w_z, w_z, conv_w, conv_b.reshape(1, width), ln_g.reshape(1, width), ln_b.reshape(1, width))


def _ffn_up_body(x_ref, xp_ref, xn_ref, g_ref, sh_ref, sc_ref, wg_ref, wv_ref, cw_ref, cb_ref, f_ref, h_ref,
                 *, tm, tps):
    _, first, last = _seq_edges(tps)

    @pl.when(pl.program_id(1) == 0)
    def _():
        def norm(x):
            ms = jnp.mean(x * x, axis=-1, keepdims=True)
            y = x * lax.rsqrt(ms + EPS) * g_ref[...]
            return (y * (1.0 + sc_ref[...]) + sh_ref[...]).astype(h_ref.dtype)

        zero = jnp.zeros((HALO, h_ref.shape[1]), h_ref.dtype)
        h_ref[0:HALO, :] = jnp.where(first, zero, norm(xp_ref[...]))
        h_ref[HALO:HALO + tm, :] = norm(x_ref[...])
        h_ref[HALO + tm:HALO + tm + HALO, :] = jnp.where(last, zero, norm(xn_ref[...]))

    gate = jnp.dot(h_ref[...], wg_ref[...], preferred_element_type=F32)
    val = jnp.dot(h_ref[HALO:HALO + tm, :], wv_ref[...], preferred_element_type=F32)
    taps = cw_ref.shape[0]
    pad = taps // 2
    conv = cw_ref[0:1, :] * gate[HALO - pad:HALO - pad + tm, :]
    for t in range(1, taps):
        conv = conv + cw_ref[t:t + 1, :] * gate[HALO - pad + t:HALO - pad + t + tm, :]
    g = conv + cb_ref[...]
    f_ref[...] = (g * _sigmoid(g) * val).astype(f_ref.dtype)


def _ffn_up(x, gain, mods, k_shift, rows_per_mod, seq_len, w_up, conv_w, conv_b):
    m, d = x.shape
    d_ff = conv_w.shape[1]
    tc = _pick(d_ff, 512)
    nj = d_ff // tc
    tm = _pick(seq_len, 1024)
    per = tm // HALO
    last_blk = m // HALO - 1
    return pl.pallas_call(
        functools.partial(_ffn_up_body, tm=tm, tps=seq_len // tm),
        grid=(m // tm, nj),
        in_specs=[
            pl.BlockSpec((tm, d), lambda i, j: (i, 0)),
            pl.BlockSpec((HALO, d), lambda i, j: (jnp.maximum(i * per - 1, 0), 0)),
            pl.BlockSpec((HALO, d), lambda i, j: (jnp.minimum((i + 1) * per, last_blk), 0)),
            pl.BlockSpec((1, d), lambda i, j: (0, 0)),
            pl.BlockSpec((None, 1, d), lambda i, j: (i * tm // rows_per_mod, 0, k_shift)),
            pl.BlockSpec((None, 1, d), lambda i, j: (i * tm // rows_per_mod, 0, k_shift + 1)),
            pl.BlockSpec((d, tc), lambda i, j: (0, j)),
            pl.BlockSpec((d, tc), lambda i, j: (0, nj + j)),
            pl.BlockSpec((conv_w.shape[0], tc), lambda i, j: (0, j)),
            pl.BlockSpec((1, tc), lambda i, j: (0, j)),
        ],
        out_specs=pl.BlockSpec((tm, tc), lambda i, j: (i, j)),
        out_shape=jax.ShapeDtypeStruct((m, d_ff), BF16),
        scratch_shapes=[pltpu.VMEM((tm + 2 * HALO, d), BF16)],
        compiler_params=_cparams("arbitrary", "arbitrary"),
        name="ffn_up",
    )(x, x, x, gain.reshape(1, d), mods, mods, w_up, w_up, conv_w, conv_b.reshape(1, d_ff))


def _rope_vreg(bn, cos_ref, sin_ref):
    return bn * cos_ref[...] + pltpu.roll(bn, LANE // 2, 1) * sin_ref[...]


def _norm_qkv_body(x_ref, g_ref, sh_ref, sc_ref, wz_ref, qg_ref, wq_ref, kvg_ref, wkv_ref, qhg_ref, khg_ref,
                   cos_ref, sin_ref, h_ref, q_ref, k_ref, v_ref, *, q_lora, kv_lora, heads, qk_dim, scale):
    x = x_ref[...]
    y = x * lax.rsqrt(jnp.mean(x * x, axis=-1, keepdims=True) + EPS) * g_ref[...]
    h = (y * (1.0 + sc_ref[...]) + sh_ref[...]).astype(h_ref.dtype)
    h_ref[...] = h
    z = jnp.dot(h, wz_ref[...], preferred_element_type=F32)
    lane = lax.broadcasted_iota(jnp.int32, (1, LANE), 1)
    rope_lanes = lane < (LANE // 2)
    slot = 2 * LANE

    def rms(v, g):
        return v * lax.rsqrt(jnp.mean(v * v, axis=-1, keepdims=True) + EPS) * g

    qn = rms(z[:, :q_lora], qg_ref[...]).astype(BF16)
    yq = jnp.dot(qn, wq_ref[...], preferred_element_type=F32)
    g_nope, g_rope = qhg_ref[:, :LANE], qhg_ref[:, LANE:]
    for h in range(heads):
        a = yq[:, h * slot:h * slot + LANE]
        b = yq[:, h * slot + LANE:(h + 1) * slot]
        ss = jnp.sum(a * a + jnp.where(rope_lanes, b * b, 0.0), axis=-1, keepdims=True)
        r = lax.rsqrt(ss / qk_dim + EPS) * scale
        q_ref[:, h * slot:h * slot + LANE] = (a * r * g_nope).astype(q_ref.dtype)
        q_ref[:, h * slot + LANE:(h + 1) * slot] = _rope_vreg(b * r * g_rope, cos_ref, sin_ref).astype(q_ref.dtype)

    kvn = rms(z[:, q_lora:q_lora + kv_lora], kvg_ref[...]).astype(BF16)
    ykv = jnp.dot(kvn, wkv_ref[...], preferred_element_type=F32)
    kr = z[:, q_lora + kv_lora:q_lora + kv_lora + LANE]
    kr_ss = jnp.sum(jnp.where(rope_lanes, kr * kr, 0.0), axis=-1, keepdims=True)
    g_nope, g_rope = khg_ref[:, :LANE], khg_ref[:, LANE:]
    for h in range(heads):
        a = ykv[:, h * LANE:(h + 1) * LANE]
        ss = jnp.sum(a * a, axis=-1, keepdims=True) + kr_ss
        r = lax.rsqrt(ss / qk_dim + EPS)
        k_ref[:, h * slot:h * slot + LANE] = (a * r * g_nope).astype(k_ref.dtype)
        k_ref[:, h * slot + LANE:(h + 1) * slot] = _rope_vreg(kr * r * g_rope, cos_ref, sin_ref).astype(k_ref.dtype)
    ones = jnp.ones((z.shape[0], LANE), v_ref.dtype)
    for h in range(heads):
        v_ref[:, h * slot:h * slot + LANE] = ykv[:, (heads + h) * LANE:(heads + h + 1) * LANE].astype(v_ref.dtype)
        v_ref[:, h * slot + LANE:(h + 1) * slot] = ones


def _norm_qkv(x, gain, mods, k_shift, rows_per_mod, lw, cos_t, sin_t, seq_len, dims, use_rope):
    m, d = x.shape
    heads, q_lora, kv_lora, qk_dim = dims["heads"], dims["q_lora"], dims["kv_lora"], dims["qk_dim"]
    w_zq = lw["w_z"][:, -dims["nq"]:]
    tm = _pick(seq_len, 512)
    tps = seq_len // tm
    if use_rope:
        tab = pl.BlockSpec((tm, LANE), lambda i: (i % tps, 0))
    else:
        tab = pl.BlockSpec((tm, LANE), lambda i: (0, 0))

    def full(a):
        return pl.BlockSpec(a.shape, lambda i: (0, 0))

    qw = heads * 2 * LANE
    row = pl.BlockSpec((tm, qw), lambda i: (i, 0))
    return pl.pallas_call(
        functools.partial(_norm_qkv_body, q_lora=q_lora, kv_lora=kv_lora, heads=heads, qk_dim=qk_dim,
                          scale=qk_dim ** -0.5 * LOG2E),
        grid=(m // tm,),
        in_specs=[
            pl.BlockSpec((tm, d), lambda i: (i, 0)),
            pl.BlockSpec((1, d), lambda i: (0, 0)),
            pl.BlockSpec((None, 1, d), lambda i: (i * tm // rows_per_mod, 0, k_shift)),
            pl.BlockSpec((None, 1, d), lambda i: (i * tm // rows_per_mod, 0, k_shift + 1)),
            full(w_zq),
            full(lw["q_norm_g"]), full(lw["wq"]), full(lw["kv_norm_g"]), full(lw["wkv"]),
            full(lw["q_head_g"]), full(lw["k_head_g"]), tab, tab,
        ],
        out_specs=[pl.BlockSpec((tm, d), lambda i: (i, 0)), row, row, row],
        out_shape=[jax.ShapeDtypeStruct((m, d), BF16)] + [jax.ShapeDtypeStruct((m, qw), BF16)] * 3,
        compiler_params=_cparams("arbitrary"),
        name="norm_qkv",
    )(x, gain.reshape(1, d), mods, mods, w_zq, lw["q_norm_g"], lw["wq"], lw["kv_norm_g"], lw["wkv"],
      lw["q_head_g"], lw["k_head_g"], cos_t, sin_t)


def _attn_update(q, k, v1, carry):
    m_i, acc = carry
    s = lax.dot_general(q, k, (((1,), (1,)), ((), ())), preferred_element_type=F32)
    m_new = jnp.maximum(m_i, jnp.max(s, axis=-1, keepdims=True))
    alpha = jnp.exp2(m_i - m_new)
    p = jnp.exp2(s - m_new)
    acc = alpha * acc + jnp.dot(p.astype(v1.dtype), v1, preferred_element_type=F32)
    return m_new, acc


def _attn_body(*refs, tk, n_lat, hps):
    if n_lat:
        q_ref, kc_ref, vc_ref, kx_ref, vx_ref, o_ref = refs
    else:
        q_ref, kc_ref, vc_ref, o_ref = refs
    tq = q_ref.shape[0]
    slot, vd = 2 * LANE, LANE
    heads = [slice(h * slot, (h + 1) * slot) for h in range(hps)]
    qs = [q_ref[:, hs] for hs in heads]
    carries = [(jnp.full((tq, 1), -jnp.inf, F32), jnp.zeros((tq, slot), F32)) for _ in heads]
    carries = [_attn_update(q, kc_ref[:, hs], vc_ref[:, hs], c) for q, hs, c in zip(qs, heads, carries)]
    for c in range(n_lat):
        rows = slice(c * tk, (c + 1) * tk)
        carries = [_attn_update(q, kx_ref[rows, hs], vx_ref[rows, hs], cr)
                   for q, hs, cr in zip(qs, heads, carries)]
    for h, (_, acc) in enumerate(carries):
        o_ref[:, h * vd:(h + 1) * vd] = (acc[:, :vd] / acc[:, vd:]).astype(o_ref.dtype)


def _attention(q, kc, vc, kx, vx, batch, heads):
    lq = q.shape[0] // batch
    lc = kc.shape[0] // batch
    vd = LANE
    hps = 2 if heads % 2 == 0 else 1
    slot = hps * 2 * LANE
    tq = _pick(lq, 1024)
    q3 = q.reshape(batch, lq, heads * 2 * LANE)
    args = [q3, kc.reshape(batch, lc, heads * 2 * LANE), vc.reshape(batch, lc, heads * 2 * LANE)]
    in_specs = [
        pl.BlockSpec((None, tq, slot), lambda b, h, i: (b, i, h)),
        pl.BlockSpec((None, lc, slot), lambda b, h, i: (b, 0, h)),
        pl.BlockSpec((None, lc, slot), lambda b, h, i: (b, 0, h)),
    ]
    n_lat, tk = 0, 0
    if kx is not None:
        lx = kx.shape[0] // batch
        tk = _pick(lx, 1024)
        n_lat = lx // tk
        args += [kx.reshape(batch, lx, heads * 2 * LANE), vx.reshape(batch, lx, heads * 2 * LANE)]
        in_specs += [
            pl.BlockSpec((None, lx, slot), lambda b, h, i: (b, 0, h)),
            pl.BlockSpec((None, lx, slot), lambda b, h, i: (b, 0, h)),
        ]
    out = pl.pallas_call(
        functools.partial(_attn_body, tk=tk, n_lat=n_lat, hps=hps),
        grid=(batch, heads // hps, lq // tq),
        in_specs=in_specs,
        out_specs=pl.BlockSpec((None, tq, hps * vd), lambda b, h, i: (b, i, h)),
        out_shape=jax.ShapeDtypeStruct((batch, lq, heads * vd), BF16),
        compiler_params=_cparams("arbitrary", "arbitrary", "arbitrary"),
        name="attention_x" if n_lat else "attention_c",
    )(*args)
    return out.reshape(batch * lq, heads * vd)


def _merge_body(h_ref, a_ref, t_ref, p_ref, d_ref, wga_ref, wgt_ref, wgp_ref, wgd_ref, wa_ref, wt_ref, wp_ref,
                ps_ref, wd_ref, o_ref):
    h = h_ref[...]

    def gate(wg_ref):
        return _sigmoid(jnp.dot(h, wg_ref[...], preferred_element_type=F32))

    def proj(x_ref, w):
        return jnp.dot(x_ref[...], w, preferred_element_type=F32)

    acc = gate(wga_ref) * proj(a_ref, wa_ref[...])
    acc = acc + gate(wgt_ref) * proj(t_ref, wt_ref[...])
    acc = acc + gate(wgp_ref) * (proj(p_ref, wp_ref[...]) * ps_ref[...])
    acc = acc + gate(wgd_ref) * proj(d_ref, wd_ref[...])
    o_ref[...] = acc.astype(o_ref.dtype)


def _merge(h, act_a, att, act_p, act_d, lw, rows_per_seq):
    m, d = h.shape
    n_branch = lw["wg"].shape[1] // d
    assert n_branch == 4 and len(POOL_WINDOWS) == n_branch
    tn = d // n_branch
    group = act_p.shape[1] // n_branch
    tm = _pick(rows_per_seq, 512)

    def act(a):
        return pl.BlockSpec((tm, a.shape[1]), lambda j, i: (i, 0))

    def wcol(w):
        return pl.BlockSpec((w.shape[0], tn), lambda j, i: (0, j))

    return pl.pallas_call(
        _merge_body,
        grid=(n_branch, m // tm),
        in_specs=[
            act(h), act(act_a), act(att),
            pl.BlockSpec((tm, group), lambda j, i: (i, j)),
            act(act_d),
            *[pl.BlockSpec((d, tn), lambda j, i, b=b: (0, b * n_branch + j)) for b in range(n_branch)],
            wcol(lw["w_a_out"]), wcol(lw["w_mla_out"]),
            pl.BlockSpec((None, group, tn), lambda j, i: (j, 0, 0)),
            pl.BlockSpec((1, tn), lambda j, i: (0, j)),
            wcol(lw["w_d_out"]),
        ],
        out_specs=pl.BlockSpec((tm, tn), lambda j, i: (i, j)),
        out_shape=jax.ShapeDtypeStruct((m, d), BF16),
        compiler_params=_cparams("arbitrary", "arbitrary"),
        name="merge",
    )(h, act_a, att, act_p, act_d, lw["wg"], lw["wg"], lw["wg"], lw["wg"], lw["w_a_out"], lw["w_mla_out"],
      lw["w_pool"], lw["pool_scale"], lw["w_d_out"])


def _rope_perm(rope_dim):
    nf = rope_dim // 4
    j = jnp.arange(rope_dim)
    return jnp.where((j % (2 * nf)) < nf, j + nf, j - nf)


def _rope_tables(seq_len, rope_dim):
    rows = seq_len // GRID_W
    row = jnp.broadcast_to(jnp.arange(rows)[:, None], (rows, GRID_W)).reshape(seq_len)
    col = jnp.broadcast_to(jnp.arange(GRID_W)[None, :], (rows, GRID_W)).reshape(seq_len)
    nf = rope_dim // 4
    inv = ROPE_THETA ** (-jnp.arange(nf, dtype=F32) / nf)
    ang = jnp.stack([row, col], axis=-1).astype(F32)[:, :, None] * inv
    cos, sin = jnp.cos(ang), jnp.sin(ang)
    cos_t = jnp.concatenate([cos, cos], axis=-1).reshape(seq_len, rope_dim)
    sin_t = jnp.concatenate([-sin, sin], axis=-1).reshape(seq_len, rope_dim)
    pad = jnp.zeros((seq_len, LANE - rope_dim), F32)
    return jnp.concatenate([cos_t, pad], axis=-1), jnp.concatenate([sin_t, pad], axis=-1)


def _layer_weights(l, p, dims):
    d, wa, wc, wd = dims["d"], dims["wa"], dims["wc"], dims["wd"]
    heads, nope, rope, vdim = dims["heads"], dims["nope"], dims["rope"], dims["vdim"]
    q_lora, kv_lora, nq = dims["q_lora"], dims["kv_lora"], dims["nq"]
    perm = _rope_perm(rope)
    w_in = p["w_in"][l].astype(BF16)
    off_q = 3 * wa
    off_kv = off_q + q_lora
    off_kr = off_kv + kv_lora
    off_p = off_kr + rope
    off_d = off_p + wc
    off_g = off_d + 2 * wd
    kr = w_in[:, off_kr:off_p]
    qkv_used = q_lora + kv_lora + 2 * rope
    w_z = jnp.concatenate([
        w_in[:, off_d:off_g],
        w_in[:, wa:3 * wa],
        w_in[:, :wa],
        w_in[:, off_p:off_d],
        w_in[:, off_q:off_kr],
        kr, kr[:, perm],
        jnp.zeros((d, nq - qkv_used), w_in.dtype),
    ], axis=1)
    wg = w_in[:, off_g:]

    qk = nope + rope
    wq = p["w_q_up"][l].astype(BF16).reshape(q_lora, heads, qk)
    wq = jnp.concatenate([wq, wq[:, :, nope:][:, :, perm]], axis=-1).reshape(q_lora, heads * 2 * LANE)
    wkv = p["w_kv_up"][l].astype(BF16).reshape(kv_lora, heads, nope + vdim)
    wkv = jnp.concatenate([wkv[:, :, :nope].reshape(kv_lora, heads * nope),
                           wkv[:, :, nope:].reshape(kv_lora, heads * vdim)], axis=1)

    def head_gain(g):
        return jnp.concatenate([g, g[nope:][perm]]).reshape(1, 2 * LANE)

    return {
        "w_z": w_z, "wg": wg,
        "wq": wq, "wkv": wkv,
        "q_norm_g": p["q_norm_g"][l].reshape(1, q_lora), "kv_norm_g": p["kv_norm_g"][l].reshape(1, kv_lora),
        "q_head_g": head_gain(p["q_head_g"][l]), "k_head_g": head_gain(p["k_head_g"][l]),
        "w_a_out": p["w_a_out"][l].astype(BF16), "w_mla_out": p["w_mla_out"][l].astype(BF16),
        "w_pool": p["w_pool"][l].astype(BF16), "pool_scale": p["pool_scale"][l].reshape(1, d),
        "w_d_out": p["w_d_out"][l].astype(BF16), "w_out": p["w_out"][l].astype(BF16),
        "w_up": p["w_up"][l].astype(BF16), "w_down": p["w_down"][l].astype(BF16),
    }


def _token_mixer_inputs(xs, mods, rows_per_mod, seq_len, l, p, lw, dims, tables):
    return _norm_qkv(xs, p["norm1_g"][l], mods, 0, rows_per_mod, lw, tables[0], tables[1], seq_len, dims, tables[2])


def _finish_layer(xs, mods, rows_per_mod, seq_len, l, p, lw, dims, h, att):
    wa, wc, wd = dims["wa"], dims["wc"], dims["wd"]
    act_a = _proj_conv_a(h, lw["w_z"], p["conv_a_w"][l], wa, seq_len)
    act_p = _proj_pool(h, lw["w_z"], wc, seq_len)
    act_d = _proj_conv_d(h, lw["w_z"], p["conv_d_w"][l], p["conv_d_b"][l], p["cd_ln_g"][l], p["cd_ln_b"][l],
                         wd, seq_len)
    merged = _merge(h, act_a, att, act_p, act_d, lw, seq_len)
    x1 = _mm_res(merged, lw["w_out"], xs, mods, 2, rows_per_mod, tm=512, tn=2048)
    f = _ffn_up(x1, p["norm2_g"][l], mods, 3, rows_per_mod, seq_len, lw["w_up"], p["conv_ff_w"][l],
                p["conv_ff_b"][l])
    return _mm_res(f, lw["w_down"], x1, mods, 5, rows_per_mod, tm=1024, tn=512)


def kernel(x, c, ctx, c_ctx, ada_w, ada_b, norm1_g, w_in, conv_a_w, w_a_out, q_norm_g, w_q_up, kv_norm_g, w_kv_up, q_head_g, k_head_g, w_mla_out, w_pool, pool_scale, conv_d_w, conv_d_b, cd_ln_g, cd_ln_b, w_d_out, w_out, norm2_g, w_up, conv_ff_w, conv_ff_b, w_down):
    p = dict(norm1_g=norm1_g, w_in=w_in, conv_a_w=conv_a_w, w_a_out=w_a_out, q_norm_g=q_norm_g, w_q_up=w_q_up,
             kv_norm_g=kv_norm_g, w_kv_up=w_kv_up, q_head_g=q_head_g, k_head_g=k_head_g, w_mla_out=w_mla_out,
             w_pool=w_pool, pool_scale=pool_scale, conv_d_w=conv_d_w, conv_d_b=conv_d_b, cd_ln_g=cd_ln_g,
             cd_ln_b=cd_ln_b, w_d_out=w_d_out, w_out=w_out, norm2_g=norm2_g, w_up=w_up, conv_ff_w=conv_ff_w,
             conv_ff_b=conv_ff_b, w_down=w_down)
    batch, seq, d = x.shape
    n_ctx = ctx.shape[1]
    depth = ada_w.shape[0]
    qk_dim = q_head_g.shape[1]
    heads = w_q_up.shape[2] // qk_dim
    vdim = w_mla_out.shape[1] // heads
    nope = w_kv_up.shape[2] // heads - vdim
    rope = qk_dim - nope
    wa, wc, wd = conv_a_w.shape[2], w_pool.shape[1] * w_pool.shape[2], conv_d_w.shape[2]
    q_lora, kv_lora = w_q_up.shape[1], w_kv_up.shape[1]
    assert nope == LANE and vdim == LANE and 2 * rope == LANE, "head layout assumes 128 | 64 | 128 dims"
    assert wa == wc == wd and wa % (len(POOL_WINDOWS) * LANE) == 0
    nq = q_lora + kv_lora + 2 * rope
    while (6 * wa) % nq or nq % LANE:
        nq += LANE
    dims = dict(d=d, wa=wa, wc=wc, wd=wd, heads=heads, nope=nope, rope=rope, vdim=vdim, qk_dim=qk_dim,
                q_lora=q_lora, kv_lora=kv_lora, nq=nq)

    rows = -(-(batch + 1) // 8) * 8
    cc = jnp.zeros((rows, d), F32).at[:batch].set(c).at[batch].set(c_ctx)
    mods = _ada(cc, ada_w, ada_b)

    cos_t, sin_t = _rope_tables(seq, rope)
    ones_t = jnp.concatenate([jnp.ones((n_ctx, rope), F32), jnp.zeros((n_ctx, LANE - rope), F32)], axis=-1)
    zeros_t = jnp.zeros((n_ctx, LANE), F32)
    tab_x = (cos_t, sin_t, True)
    tab_c = (ones_t, zeros_t, False)

    xs = x.reshape(batch * seq, d)
    cs = ctx.reshape(batch * n_ctx, d)
    for l in range(depth):
        last = l == depth - 1
        lw = _layer_weights(l, p, dims)
        mods_x = mods[l, :batch].reshape(batch, 1, -1)
        mods_c = mods[l, batch:batch + 1].reshape(1, 1, -1)
        hc, qc, kc, vc = _token_mixer_inputs(cs, mods_c, batch * n_ctx, n_ctx, l, p, lw, dims, tab_c)
        hx, qx, kx, vx = _token_mixer_inputs(xs, mods_x, seq, seq, l, p, lw, dims, tab_x)
        att_x = _attention(qx, kc, vc, kx, vx, batch, heads)
        xs_new = _finish_layer(xs, mods_x, seq, seq, l, p, lw, dims, hx, att_x)
        if not last:
            att_c = _attention(qc, kc, vc, None, None, batch, heads)
            cs = _finish_layer(cs, mods_c, batch * n_ctx, n_ctx, l, p, lw, dims, hc, att_c)
        xs = xs_new
    return xs.reshape(batch, seq, d)
```

```python
import functools

import jax
import jax.numpy as jnp
from jax import lax
from jax.experimental import pallas as pl
from jax.experimental.pallas import tpu as pltpu

GRID_W = 64
ROPE_THETA = 10000.0
POOL_WINDOWS = (2, 4, 8, 16)
EPS = 1e-6
LN_EPS = 1e-5

LOG2E = 1.4426950408889634
LANE = 128
SUBLANES = 8
HALO = 16
VMEM_LIMIT = 56 * 1024 * 1024

F32 = jnp.float32
BF16 = jnp.bfloat16


def _cparams(*sem):
    return pltpu.CompilerParams(dimension_semantics=sem, vmem_limit_bytes=VMEM_LIMIT)


def _sigmoid(x):
    return 0.5 * jnp.tanh(0.5 * x) + 0.5


def _pick(n, pref):
    if n <= pref:
        return n
    t = pref
    while n % t:
        t //= 2
    return t


def _ada_body(cc_ref, w_ref, b_ref, o_ref):
    cc = cc_ref[...]
    s = (cc * jax.nn.sigmoid(cc)).astype(BF16)
    o_ref[...] = jnp.dot(s, w_ref[...].astype(BF16), preferred_element_type=F32) + b_ref[...]


def _ada(cc, ada_w, ada_b):
    depth, d, n = ada_w.shape
    rows = cc.shape[0]
    tn = _pick(n, 1024)
    return pl.pallas_call(
        _ada_body,
        grid=(depth, n // tn),
        in_specs=[
            pl.BlockSpec((rows, d), lambda l, j: (0, 0)),
            pl.BlockSpec((None, d, tn), lambda l, j: (l, 0, j)),
            pl.BlockSpec((None, 1, tn), lambda l, j: (l, 0, j)),
        ],
        out_specs=pl.BlockSpec((None, rows, tn), lambda l, j: (l, 0, j)),
        out_shape=jax.ShapeDtypeStruct((depth, rows, n), F32),
        compiler_params=_cparams("arbitrary", "arbitrary"),
        name="ada",
    )(cc, ada_w, ada_b.reshape(depth, 1, n))


def _mm_res_body(a_ref, w_ref, r_ref, gate_ref, o_ref):
    acc = jnp.dot(a_ref[...], w_ref[...], preferred_element_type=F32)
    o_ref[...] = r_ref[...] + gate_ref[...] * acc


def _mm_res(a, w, res, mods, k_gate, rows_per_mod, *, tm, tn):
    m, k = a.shape
    n = w.shape[1]
    tm = _pick(rows_per_mod, tm)
    tn = _pick(n, tn)
    nj = n // tn
    return pl.pallas_call(
        _mm_res_body,
        grid=(m // tm, nj),
        in_specs=[
            pl.BlockSpec((tm, k), lambda i, j: (i, 0)),
            pl.BlockSpec((k, tn), lambda i, j: (0, j)),
            pl.BlockSpec((tm, tn), lambda i, j: (i, j)),
            pl.BlockSpec((None, 1, tn), lambda i, j: (i * tm // rows_per_mod, 0, k_gate * nj + j)),
        ],
        out_specs=pl.BlockSpec((tm, tn), lambda i, j: (i, j)),
        out_shape=jax.ShapeDtypeStruct((m, n), F32),
        compiler_params=_cparams("arbitrary", "arbitrary"),
        name="mm_res",
    )(a, w, res, mods)


def _halo_specs(tl, width, col_block, n_rows):
    per = tl // HALO
    last = n_rows // HALO - 1
    prev = pl.BlockSpec((HALO, width), lambda i, *_: (jnp.maximum(i * per - 1, 0), col_block))
    nxt = pl.BlockSpec((HALO, width), lambda i, *_: (jnp.minimum((i + 1) * per, last), col_block))
    return prev, nxt


def _seq_edges(tps):
    s = pl.program_id(0) % tps
    return s, s == 0, s == tps - 1


def _fill_lhs(lhs_ref, hp_ref, h_ref, hn_ref, first, last, tm):
    zero = jnp.zeros((HALO, lhs_ref.shape[1]), lhs_ref.dtype)
    lhs_ref[0:HALO, :] = jnp.where(first, zero, hp_ref[...])
    lhs_ref[HALO:HALO + tm, :] = h_ref[...]
    lhs_ref[HALO + tm:HALO + tm + HALO, :] = jnp.where(last, zero, hn_ref[...])


def _h_specs(tm, d, m):
    prev, nxt = _halo_specs(tm, d, 0, m)
    return [pl.BlockSpec((tm, d), lambda i, *_: (i, 0)), prev, nxt]


def _proj_conv_a_body(h_ref, hp_ref, hn_ref, wcg_ref, whh_ref, wb_ref, cw_ref, o_ref, lhs_ref, *, tm, tps):
    _, first, last = _seq_edges(tps)

    @pl.when(pl.program_id(1) == 0)
    def _():
        _fill_lhs(lhs_ref, hp_ref, h_ref, hn_ref, first, last, tm)

    lhs = lhs_ref[...]
    u = (jnp.dot(lhs, wcg_ref[...], preferred_element_type=F32)
         * jnp.dot(lhs, whh_ref[...], preferred_element_type=F32))
    b = jnp.dot(lhs_ref[HALO:HALO + tm, :], wb_ref[...], preferred_element_type=F32)
    taps = cw_ref.shape[0]
    pad = taps // 2
    conv = cw_ref[0:1, :] * u[HALO - pad:HALO - pad + tm, :]
    for t in range(1, taps):
        conv = conv + cw_ref[t:t + 1, :] * u[HALO - pad + t:HALO - pad + t + tm, :]
    o_ref[...] = (b * conv).astype(o_ref.dtype)


def _proj_conv_a(h, w_z, conv_w, width, seq_len):
    m, d = h.shape
    tm = _pick(seq_len, 1024)
    tc = _pick(width, 512)
    nb = width // tc

    def wcol(group):
        return pl.BlockSpec((d, tc), lambda i, j: (0, group * nb + j))

    return pl.pallas_call(
        functools.partial(_proj_conv_a_body, tm=tm, tps=seq_len // tm),
        grid=(m // tm, nb),
        in_specs=_h_specs(tm, d, m) + [wcol(2), wcol(3), wcol(4),
                                       pl.BlockSpec((conv_w.shape[0], tc), lambda i, j: (0, j))],
        out_specs=pl.BlockSpec((tm, tc), lambda i, j: (i, j)),
        out_shape=jax.ShapeDtypeStruct((m, width), BF16),
        scratch_shapes=[pltpu.VMEM((tm + 2 * HALO, d), BF16)],
        compiler_params=_cparams("arbitrary", "arbitrary"),
        name="proj_conv_a",
    )(h, h, h, w_z, w_z, w_z, conv_w)


def _window_sum(ext, win, tm):
    half = win // 2
    part, length, span = ext, ext.shape[0], 1
    while span < half:
        length -= SUBLANES
        part = part[0:length, :] + part[span:span + length, :]
        span *= 2
    return part[HALO - half:HALO - half + tm, :] + part[HALO:HALO + tm, :]


def _proj_pool_body(h_ref, hp_ref, hn_ref, w_ref, o_ref, lhs_ref, *, tm, tps, seq_len, group):
    s, first, last = _seq_edges(tps)
    _fill_lhs(lhs_ref, hp_ref, h_ref, hn_ref, first, last, tm)
    u = jnp.dot(lhs_ref[...], w_ref[...], preferred_element_type=F32)
    u = jnp.concatenate([u, jnp.zeros((SUBLANES, u.shape[1]), F32)], axis=0)
    pos = s * tm + lax.broadcasted_iota(jnp.int32, (tm, 1), 0)
    for g, win in enumerate(POOL_WINDOWS):
        cols = slice(g * group, (g + 1) * group)
        ext = u[:, cols]
        lo = jnp.maximum(pos - win // 2, 0)
        hi = jnp.minimum(pos - win // 2 + win, seq_len)
        cnt = (hi - lo).astype(F32)
        o_ref[:, cols] = (_window_sum(ext, win, tm) / cnt - ext[HALO:HALO + tm, :]).astype(o_ref.dtype)


def _proj_pool(h, w_z, width, seq_len):
    m, d = h.shape
    assert max(POOL_WINDOWS) // 2 <= SUBLANES and max(POOL_WINDOWS) <= HALO
    tm = _pick(seq_len, 512)
    return pl.pallas_call(
        functools.partial(_proj_pool_body, tm=tm, tps=seq_len // tm, seq_len=seq_len,
                          group=width // len(POOL_WINDOWS)),
        grid=(m // tm,),
        in_specs=_h_specs(tm, d, m) + [pl.BlockSpec((d, width), lambda i: (0, 5))],
        out_specs=pl.BlockSpec((tm, width), lambda i: (i, 0)),
        out_shape=jax.ShapeDtypeStruct((m, width), BF16),
        scratch_shapes=[pltpu.VMEM((tm + 2 * HALO, d), BF16)],
        compiler_params=_cparams("arbitrary"),
        name="proj_pool",
    )(h, h, h, w_z)


def _proj_conv_d_body(h_ref, hp_ref, hn_ref, wa_ref, wgt_ref, w_ref, cb_ref, lg_ref, lb_ref, o_ref,
                      lhs_ref, ext_ref, sh_ref, *, tl, tps):
    _, first, last = _seq_edges(tps)
    _fill_lhs(lhs_ref, hp_ref, h_ref, hn_ref, first, last, tl)
    lhs = lhs_ref[...]
    rows = sh_ref.shape[1]
    taps = w_ref.shape[0]
    pad = taps // 2
    slab = 2 * LANE
    ys = []
    for c0 in range(0, w_ref.shape[1], slab):
        cols = slice(c0, c0 + slab)
        ext_ref[:, cols] = (jnp.dot(lhs, wa_ref[:, cols], preferred_element_type=F32)
                            * _sigmoid(jnp.dot(lhs, wgt_ref[:, cols], preferred_element_type=F32)))
        for r in range(SUBLANES - 1):
            sh_ref[r, :, cols] = ext_ref[r + 1:r + 1 + rows, cols]
        acc = None
        for k in range(taps):
            off = HALO - pad + k
            if off % SUBLANES == 0:
                tap = ext_ref[off:off + tl, cols]
            else:
                base = off - off % SUBLANES
                tap = sh_ref[off % SUBLANES - 1, base:base + tl, cols]
            term = w_ref[k:k + 1, cols] * tap
            acc = term if acc is None else acc + term
        ys.append(acc)
    y = jnp.concatenate(ys, axis=1) + cb_ref[...]
    mu = jnp.mean(y, axis=-1, keepdims=True)
    yc = y - mu
    var = jnp.mean(yc * yc, axis=-1, keepdims=True)
    yn = yc * lax.rsqrt(var + LN_EPS) * lg_ref[...] + lb_ref[...]
    o_ref[...] = (yn * _sigmoid(yn)).astype(o_ref.dtype)


def _proj_conv_d(h, w_z, conv_w, conv_b, ln_g, ln_b, width, seq_len):
    m, d = h.shape
    tl = _pick(seq_len, 256)
    vec = pl.BlockSpec((1, width), lambda i: (0, 0))
    return pl.pallas_call(
        functools.partial(_proj_conv_d_body, tl=tl, tps=seq_len // tl),
        grid=(m // tl,),
        in_specs=_h_specs(tl, d, m) + [
            pl.BlockSpec((d, width), lambda i: (0, 0)),
            pl.BlockSpec((d, width), lambda i: (0, 1)),
            pl.BlockSpec(conv_w.shape, lambda i: (0, 0)),
            vec, vec, vec,
        ],
        out_specs=pl.BlockSpec((tl, width), lambda i: (i, 0)),
        out_shape=jax.ShapeDtypeStruct((m, width), BF16),
        scratch_shapes=[pltpu.VMEM((tl + 2 * HALO, d), BF16),
                        pltpu.VMEM((tl + 2 * HALO, width), F32),
                        pltpu.VMEM((SUBLANES - 1, tl + 2 * HALO - SUBLANES, width), F32)],
        compiler_params=_cparams("arbitrary"),
        name="proj_conv_d",
    )(h, h, h, w_z, w_z, conv_w, conv_b.reshape(1, width), ln_g.reshape(1, width), ln_b.reshape(1, width))


def _ffn_up_body(x_ref, xp_ref, xn_ref, g_ref, sh_ref, sc_ref, wg_ref, wv_ref, cw_ref, cb_ref, f_ref, h_ref,
                 *, tm, tps, seq_len):
    _, first, last = _seq_edges(tps)

    @pl.when(pl.program_id(1) == 0)
    def _():
        def norm(x):
            ms = jnp.mean(x * x, axis=-1, keepdims=True)
            y = x * lax.rsqrt(ms + EPS) * g_ref[...]
            return (y * (1.0 + sc_ref[...]) + sh_ref[...]).astype(h_ref.dtype)

        zero = jnp.zeros((HALO, h_ref.shape[1]), h_ref.dtype)
        h_ref[0:HALO, :] = jnp.where(first, zero, norm(xp_ref[...]))
        h_ref[HALO:HALO + tm, :] = norm(x_ref[...])
        h_ref[HALO + tm:HALO + tm + HALO, :] = jnp.where(last, zero, norm(xn_ref[...]))

    gate = jnp.dot(h_ref[...], wg_ref[...], preferred_element_type=F32)
    val = jnp.dot(h_ref[HALO:HALO + tm, :], wv_ref[...], preferred_element_type=F32)
    taps = cw_ref.shape[0]
    pad = taps // 2
    pos = lax.broadcasted_iota(jnp.int32, (tm, 1), 0) % seq_len
    conv = None
    for t in range(taps):
        tap = gate[HALO - pad + t:HALO - pad + t + tm, :]
        if tm > seq_len and t != pad:
            src = pos + (t - pad)
            tap = jnp.where((src >= 0) & (src < seq_len), tap, 0.0)
        term = cw_ref[t:t + 1, :] * tap
        conv = term if conv is None else conv + term
    g = conv + cb_ref[...]
    f_ref[...] = (g * _sigmoid(g) * val).astype(f_ref.dtype)


def _ffn_up(x, gain, mods, k_shift, rows_per_mod, seq_len, w_up, conv_w, conv_b):
    m, d = x.shape
    d_ff = conv_w.shape[1]
    tc = _pick(d_ff, 512)
    nj = d_ff // tc
    tm = _pick(seq_len, 1024)
    if tm == seq_len:
        span = max(s for s in range(1, 1024 // tm + 1) if rows_per_mod % (s * tm) == 0)
        tm *= span
    per = tm // HALO
    last_blk = m // HALO - 1
    return pl.pallas_call(
        functools.partial(_ffn_up_body, tm=tm, tps=max(seq_len // tm, 1), seq_len=seq_len),
        grid=(m // tm, nj),
        in_specs=[
            pl.BlockSpec((tm, d), lambda i, j: (i, 0)),
            pl.BlockSpec((HALO, d), lambda i, j: (jnp.maximum(i * per - 1, 0), 0)),
            pl.BlockSpec((HALO, d), lambda i, j: (jnp.minimum((i + 1) * per, last_blk), 0)),
            pl.BlockSpec((1, d), lambda i, j: (0, 0)),
            pl.BlockSpec((None, 1, d), lambda i, j: (i * tm // rows_per_mod, 0, k_shift)),
            pl.BlockSpec((None, 1, d), lambda i, j: (i * tm // rows_per_mod, 0, k_shift + 1)),
            pl.BlockSpec((d, tc), lambda i, j: (0, j)),
            pl.BlockSpec((d, tc), lambda i, j: (0, nj + j)),
            pl.BlockSpec((conv_w.shape[0], tc), lambda i, j: (0, j)),
            pl.BlockSpec((1, tc), lambda i, j: (0, j)),
        ],
        out_specs=pl.BlockSpec((tm, tc), lambda i, j: (i, j)),
        out_shape=jax.ShapeDtypeStruct((m, d_ff), BF16),
        scratch_shapes=[pltpu.VMEM((tm + 2 * HALO, d), BF16)],
        compiler_params=_cparams("arbitrary", "arbitrary"),
        name="ffn_up",
    )(x, x, x, gain.reshape(1, d), mods, mods, w_up, w_up, conv_w, conv_b.reshape(1, d_ff))


def _rope_vreg(bn, cos_ref, sin_ref):
    return bn * cos_ref[...] + pltpu.roll(bn, LANE // 2, 1) * sin_ref[...]


def _norm_qkv_body(x_ref, g_ref, sh_ref, sc_ref, wz_ref, qg_ref, wq_ref, kvg_ref, wkv_ref, qhg_ref, khg_ref,
                   cos_ref, sin_ref, h_ref, q_ref, k_ref, v_ref, *, q_lora, kv_lora, heads, qk_dim, scale):
    x = x_ref[...]
    y = x * lax.rsqrt(jnp.mean(x * x, axis=-1, keepdims=True) + EPS) * g_ref[...]
    h = (y * (1.0 + sc_ref[...]) + sh_ref[...]).astype(h_ref.dtype)
    h_ref[...] = h
    z = jnp.dot(h, wz_ref[...], preferred_element_type=F32)
    lane = lax.broadcasted_iota(jnp.int32, (1, LANE), 1)
    rope_lanes = lane < (LANE // 2)
    slot = 2 * LANE

    def rms(v, g):
        return v * lax.rsqrt(jnp.mean(v * v, axis=-1, keepdims=True) + EPS) * g

    qn = rms(z[:, :q_lora], qg_ref[...]).astype(BF16)
    yq = jnp.dot(qn, wq_ref[...], preferred_element_type=F32)
    g_nope, g_rope = qhg_ref[:, :LANE], qhg_ref[:, LANE:]
    for h in range(heads):
        a = yq[:, h * slot:h * slot + LANE]
        b = yq[:, h * slot + LANE:(h + 1) * slot]
        ss = jnp.sum(a * a + jnp.where(rope_lanes, b * b, 0.0), axis=-1, keepdims=True)
        r = lax.rsqrt(ss / qk_dim + EPS) * scale
        q_ref[:, h * slot:h * slot + LANE] = (a * r * g_nope).astype(q_ref.dtype)
        q_ref[:, h * slot + LANE:(h + 1) * slot] = _rope_vreg(b * r * g_rope, cos_ref, sin_ref).astype(q_ref.dtype)

    kvn = rms(z[:, q_lora:q_lora + kv_lora], kvg_ref[...]).astype(BF16)
    ykv = jnp.dot(kvn, wkv_ref[...], preferred_element_type=F32)
    kr = z[:, q_lora + kv_lora:q_lora + kv_lora + LANE]
    kr_ss = jnp.sum(jnp.where(rope_lanes, kr * kr, 0.0), axis=-1, keepdims=True)
    g_nope, g_rope = khg_ref[:, :LANE], khg_ref[:, LANE:]
    for h in range(heads):
        a = ykv[:, h * LANE:(h + 1) * LANE]
        ss = jnp.sum(a * a, axis=-1, keepdims=True) + kr_ss
        r = lax.rsqrt(ss / qk_dim + EPS)
        k_ref[:, h * slot:h * slot + LANE] = (a * r * g_nope).astype(k_ref.dtype)
        k_ref[:, h * slot + LANE:(h + 1) * slot] = _rope_vreg(kr * r * g_rope, cos_ref, sin_ref).astype(k_ref.dtype)
    ones = jnp.ones((z.shape[0], LANE), v_ref.dtype)
    for h in range(heads):
        v_ref[:, h * slot:h * slot + LANE] = ykv[:, (heads + h) * LANE:(heads + h + 1) * LANE].astype(v_ref.dtype)
        v_ref[:, h * slot + LANE:(h + 1) * slot] = ones


def _norm_qkv(x, gain, mods, k_shift, rows_per_mod, lw, cos_t, sin_t, seq_len, dims, use_rope):
    m, d = x.shape
    heads, q_lora, kv_lora, qk_dim = dims["heads"], dims["q_lora"], dims["kv_lora"], dims["qk_dim"]
    w_zq = lw["w_z"][:, -dims["nq"]:]
    tm = _pick(seq_len, 512)
    tps = seq_len // tm
    if use_rope:
        tab = pl.BlockSpec((tm, LANE), lambda i: (i % tps, 0))
    else:
        tab = pl.BlockSpec((tm, LANE), lambda i: (0, 0))

    def full(a):
        return pl.BlockSpec(a.shape, lambda i: (0, 0))

    qw = heads * 2 * LANE
    row = pl.BlockSpec((tm, qw), lambda i: (i, 0))
    return pl.pallas_call(
        functools.partial(_norm_qkv_body, q_lora=q_lora, kv_lora=kv_lora, heads=heads, qk_dim=qk_dim,
                          scale=qk_dim ** -0.5 * LOG2E),
        grid=(m // tm,),
        in_specs=[
            pl.BlockSpec((tm, d), lambda i: (i, 0)),
            pl.BlockSpec((1, d), lambda i: (0, 0)),
            pl.BlockSpec((None, 1, d), lambda i: (i * tm // rows_per_mod, 0, k_shift)),
            pl.BlockSpec((None, 1, d), lambda i: (i * tm // rows_per_mod, 0, k_shift + 1)),
            full(w_zq),
            full(lw["q_norm_g"]), full(lw["wq"]), full(lw["kv_norm_g"]), full(lw["wkv"]),
            full(lw["q_head_g"]), full(lw["k_head_g"]), tab, tab,
        ],
        out_specs=[pl.BlockSpec((tm, d), lambda i: (i, 0)), row, row, row],
        out_shape=[jax.ShapeDtypeStruct((m, d), BF16)] + [jax.ShapeDtypeStruct((m, qw), BF16)] * 3,
        compiler_params=_cparams("arbitrary"),
        name="norm_qkv",
    )(x, gain.reshape(1, d), mods, mods, w_zq, lw["q_norm_g"], lw["wq"], lw["kv_norm_g"], lw["wkv"],
      lw["q_head_g"], lw["k_head_g"], cos_t, sin_t)


def _attn_update(q, k, v1, carry):
    m_i, acc = carry
    s = lax.dot_general(q, k, (((1,), (1,)), ((), ())), preferred_element_type=F32)
    m_new = jnp.maximum(m_i, jnp.max(s, axis=-1, keepdims=True))
    alpha = jnp.exp2(m_i - m_new)
    p = jnp.exp2(s - m_new)
    acc = alpha * acc + jnp.dot(p.astype(v1.dtype), v1, preferred_element_type=F32)
    return m_new, acc


def _attn_body(*refs, tk, n_lat, hps):
    if n_lat:
        q_ref, kc_ref, vc_ref, kx_ref, vx_ref, o_ref = refs
    else:
        q_ref, kc_ref, vc_ref, o_ref = refs
    tq = q_ref.shape[0]
    slot, vd = 2 * LANE, LANE
    heads = [slice(h * slot, (h + 1) * slot) for h in range(hps)]
    qs = [q_ref[:, hs] for hs in heads]
    carries = [(jnp.full((tq, 1), -jnp.inf, F32), jnp.zeros((tq, slot), F32)) for _ in heads]
    carries = [_attn_update(q, kc_ref[:, hs], vc_ref[:, hs], c) for q, hs, c in zip(qs, heads, carries)]
    for c in range(n_lat):
        rows = slice(c * tk, (c + 1) * tk)
        carries = [_attn_update(q, kx_ref[rows, hs], vx_ref[rows, hs], cr)
                   for q, hs, cr in zip(qs, heads, carries)]
    for h, (_, acc) in enumerate(carries):
        o_ref[:, h * vd:(h + 1) * vd] = (acc[:, :vd] / acc[:, vd:]).astype(o_ref.dtype)


def _attention(q, kc, vc, kx, vx, batch, heads):
    lq = q.shape[0] // batch
    lc = kc.shape[0] // batch
    vd = LANE
    hps = 2 if heads % 2 == 0 else 1
    slot = hps * 2 * LANE
    tq = _pick(lq, 1024)
    q3 = q.reshape(batch, lq, heads * 2 * LANE)
    args = [q3, kc.reshape(batch, lc, heads * 2 * LANE), vc.reshape(batch, lc, heads * 2 * LANE)]
    in_specs = [
        pl.BlockSpec((None, tq, slot), lambda b, h, i: (b, i, h)),
        pl.BlockSpec((None, lc, slot), lambda b, h, i: (b, 0, h)),
        pl.BlockSpec((None, lc, slot), lambda b, h, i: (b, 0, h)),
    ]
    n_lat, tk = 0, 0
    if kx is not None:
        lx = kx.shape[0] // batch
        tk = _pick(lx, 1024)
        n_lat = lx // tk
        args += [kx.reshape(batch, lx, heads * 2 * LANE), vx.reshape(batch, lx, heads * 2 * LANE)]
        in_specs += [
            pl.BlockSpec((None, lx, slot), lambda b, h, i: (b, 0, h)),
            pl.BlockSpec((None, lx, slot), lambda b, h, i: (b, 0, h)),
        ]
    out = pl.pallas_call(
        functools.partial(_attn_body, tk=tk, n_lat=n_lat, hps=hps),
        grid=(batch, heads // hps, lq // tq),
        in_specs=in_specs,
        out_specs=pl.BlockSpec((None, tq, hps * vd), lambda b, h, i: (b, i, h)),
        out_shape=jax.ShapeDtypeStruct((batch, lq, heads * vd), BF16),
        compiler_params=_cparams("arbitrary", "arbitrary", "arbitrary"),
        name="attention_x" if n_lat else "attention_c",
    )(*args)
    return out.reshape(batch * lq, heads * vd)


def _merge_body(h_ref, a_ref, t_ref, p_ref, d_ref, wga_ref, wgt_ref, wgp_ref, wgd_ref, wa_ref, wt_ref, wp_ref,
                ps_ref, wd_ref, o_ref):
    h = h_ref[...]

    def gate(wg_ref):
        return _sigmoid(jnp.dot(h, wg_ref[...], preferred_element_type=F32))

    def proj(x_ref, w):
        return jnp.dot(x_ref[...], w, preferred_element_type=F32)

    acc = gate(wga_ref) * proj(a_ref, wa_ref[...])
    acc = acc + gate(wgt_ref) * proj(t_ref, wt_ref[...])
    acc = acc + gate(wgp_ref) * (proj(p_ref, wp_ref[...]) * ps_ref[...])
    acc = acc + gate(wgd_ref) * proj(d_ref, wd_ref[...])
    o_ref[...] = acc.astype(o_ref.dtype)


def _merge(h, act_a, att, act_p, act_d, lw, rows_per_seq):
    m, d = h.shape
    n_branch = lw["wg"].shape[1] // d
    assert n_branch == 4 and len(POOL_WINDOWS) == n_branch
    tn = d // n_branch
    group = act_p.shape[1] // n_branch
    tm = _pick(rows_per_seq, 512)

    def act(a):
        return pl.BlockSpec((tm, a.shape[1]), lambda j, i: (i, 0))

    def wcol(w):
        return pl.BlockSpec((w.shape[0], tn), lambda j, i: (0, j))

    return pl.pallas_call(
        _merge_body,
        grid=(n_branch, m // tm),
        in_specs=[
            act(h), act(act_a), act(att),
            pl.BlockSpec((tm, group), lambda j, i: (i, j)),
            act(act_d),
            *[pl.BlockSpec((d, tn), lambda j, i, b=b: (0, b * n_branch + j)) for b in range(n_branch)],
            wcol(lw["w_a_out"]), wcol(lw["w_mla_out"]),
            pl.BlockSpec((None, group, tn), lambda j, i: (j, 0, 0)),
            pl.BlockSpec((1, tn), lambda j, i: (0, j)),
            wcol(lw["w_d_out"]),
        ],
        out_specs=pl.BlockSpec((tm, tn), lambda j, i: (i, j)),
        out_shape=jax.ShapeDtypeStruct((m, d), BF16),
        compiler_params=_cparams("arbitrary", "arbitrary"),
        name="merge",
    )(h, act_a, att, act_p, act_d, lw["wg"], lw["wg"], lw["wg"], lw["wg"], lw["w_a_out"], lw["w_mla_out"],
      lw["w_pool"], lw["pool_scale"], lw["w_d_out"])


def _rope_perm(rope_dim):
    nf = rope_dim // 4
    j = jnp.arange(rope_dim)
    return jnp.where((j % (2 * nf)) < nf, j + nf, j - nf)


def _rope_tables(seq_len, rope_dim):
    rows = seq_len // GRID_W
    row = jnp.broadcast_to(jnp.arange(rows)[:, None], (rows, GRID_W)).reshape(seq_len)
    col = jnp.broadcast_to(jnp.arange(GRID_W)[None, :], (rows, GRID_W)).reshape(seq_len)
    nf = rope_dim // 4
    inv = ROPE_THETA ** (-jnp.arange(nf, dtype=F32) / nf)
    ang = jnp.stack([row, col], axis=-1).astype(F32)[:, :, None] * inv
    cos, sin = jnp.cos(ang), jnp.sin(ang)
    cos_t = jnp.concatenate([cos, cos], axis=-1).reshape(seq_len, rope_dim)
    sin_t = jnp.concatenate([-sin, sin], axis=-1).reshape(seq_len, rope_dim)
    pad = jnp.zeros((seq_len, LANE - rope_dim), F32)
    return jnp.concatenate([cos_t, pad], axis=-1), jnp.concatenate([sin_t, pad], axis=-1)


def _layer_weights(l, p, dims):
    d, wa, wc, wd = dims["d"], dims["wa"], dims["wc"], dims["wd"]
    heads, nope, rope, vdim = dims["heads"], dims["nope"], dims["rope"], dims["vdim"]
    q_lora, kv_lora, nq = dims["q_lora"], dims["kv_lora"], dims["nq"]
    perm = _rope_perm(rope)
    w_in = p["w_in"][l].astype(BF16)
    off_q = 3 * wa
    off_kv = off_q + q_lora
    off_kr = off_kv + kv_lora
    off_p = off_kr + rope
    off_d = off_p + wc
    off_g = off_d + 2 * wd
    kr = w_in[:, off_kr:off_p]
    qkv_used = q_lora + kv_lora + 2 * rope
    w_z = jnp.concatenate([
        w_in[:, off_d:off_g],
        w_in[:, wa:3 * wa],
        w_in[:, :wa],
        w_in[:, off_p:off_d],
        w_in[:, off_q:off_kr],
        kr, kr[:, perm],
        jnp.zeros((d, nq - qkv_used), w_in.dtype),
    ], axis=1)
    wg = w_in[:, off_g:]

    qk = nope + rope
    wq = p["w_q_up"][l].astype(BF16).reshape(q_lora, heads, qk)
    wq = jnp.concatenate([wq, wq[:, :, nope:][:, :, perm]], axis=-1).reshape(q_lora, heads * 2 * LANE)
    wkv = p["w_kv_up"][l].astype(BF16).reshape(kv_lora, heads, nope + vdim)
    wkv = jnp.concatenate([wkv[:, :, :nope].reshape(kv_lora, heads * nope),
                           wkv[:, :, nope:].reshape(kv_lora, heads * vdim)], axis=1)

    def head_gain(g):
        return jnp.concatenate([g, g[nope:][perm]]).reshape(1, 2 * LANE)

    return {
        "w_z": w_z, "wg": wg,
        "wq": wq, "wkv": wkv,
        "q_norm_g": p["q_norm_g"][l].reshape(1, q_lora), "kv_norm_g": p["kv_norm_g"][l].reshape(1, kv_lora),
        "q_head_g": head_gain(p["q_head_g"][l]), "k_head_g": head_gain(p["k_head_g"][l]),
        "w_a_out": p["w_a_out"][l].astype(BF16), "w_mla_out": p["w_mla_out"][l].astype(BF16),
        "w_pool": p["w_pool"][l].astype(BF16), "pool_scale": p["pool_scale"][l].reshape(1, d),
        "w_d_out": p["w_d_out"][l].astype(BF16), "w_out": p["w_out"][l].astype(BF16),
        "w_up": p["w_up"][l].astype(BF16), "w_down": p["w_down"][l].astype(BF16),
    }


def _token_mixer_inputs(xs, mods, rows_per_mod, seq_len, l, p, lw, dims, tables):
    return _norm_qkv(xs, p["norm1_g"][l], mods, 0, rows_per_mod, lw, tables[0], tables[1], seq_len, dims, tables[2])


def _finish_layer(xs, mods, rows_per_mod, seq_len, l, p, lw, dims, h, att):
    wa, wc, wd = dims["wa"], dims["wc"], dims["wd"]
    act_a = _proj_conv_a(h, lw["w_z"], p["conv_a_w"][l], wa, seq_len)
    act_p = _proj_pool(h, lw["w_z"], wc, seq_len)
    act_d = _proj_conv_d(h, lw["w_z"], p["conv_d_w"][l], p["conv_d_b"][l], p["cd_ln_g"][l], p["cd_ln_b"][l],
                         wd, seq_len)
    merged = _merge(h, act_a, att, act_p, act_d, lw, rows_per_mod)
    x1 = _mm_res(merged, lw["w_out"], xs, mods, 2, rows_per_mod, tm=512, tn=2048)
    f = _ffn_up(x1, p["norm2_g"][l], mods, 3, rows_per_mod, seq_len, lw["w_up"], p["conv_ff_w"][l],
                p["conv_ff_b"][l])
    return _mm_res(f, lw["w_down"], x1, mods, 5, rows_per_mod, tm=1024, tn=512)


def kernel(x, c, ctx, c_ctx, ada_w, ada_b, norm1_g, w_in, conv_a_w, w_a_out, q_norm_g, w_q_up, kv_norm_g, w_kv_up, q_head_g, k_head_g, w_mla_out, w_pool, pool_scale, conv_d_w, conv_d_b, cd_ln_g, cd_ln_b, w_d_out, w_out, norm2_g, w_up, conv_ff_w, conv_ff_b, w_down):
    p = dict(norm1_g=norm1_g, w_in=w_in, conv_a_w=conv_a_w, w_a_out=w_a_out, q_norm_g=q_norm_g, w_q_up=w_q_up,
             kv_norm_g=kv_norm_g, w_kv_up=w_kv_up, q_head_g=q_head_g, k_head_g=k_head_g, w_mla_out=w_mla_out,
             w_pool=w_pool, pool_scale=pool_scale, conv_d_w=conv_d_w, conv_d_b=conv_d_b, cd_ln_g=cd_ln_g,
             cd_ln_b=cd_ln_b, w_d_out=w_d_out, w_out=w_out, norm2_g=norm2_g, w_up=w_up, conv_ff_w=conv_ff_w,
             conv_ff_b=conv_ff_b, w_down=w_down)
    batch, seq, d = x.shape
    n_ctx = ctx.shape[1]
    depth = ada_w.shape[0]
    qk_dim = q_head_g.shape[1]
    heads = w_q_up.shape[2] // qk_dim
    vdim = w_mla_out.shape[1] // heads
    nope = w_kv_up.shape[2] // heads - vdim
    rope = qk_dim - nope
    wa, wc, wd = conv_a_w.shape[2], w_pool.shape[1] * w_pool.shape[2], conv_d_w.shape[2]
    q_lora, kv_lora = w_q_up.shape[1], w_kv_up.shape[1]
    assert nope == LANE and vdim == LANE and 2 * rope == LANE, "head layout assumes 128 | 64 | 128 dims"
    assert wa == wc == wd and wa % (len(POOL_WINDOWS) * LANE) == 0
    nq = q_lora + kv_lora + 2 * rope
    while (6 * wa) % nq or nq % LANE:
        nq += LANE
    dims = dict(d=d, wa=wa, wc=wc, wd=wd, heads=heads, nope=nope, rope=rope, vdim=vdim, qk_dim=qk_dim,
                q_lora=q_lora, kv_lora=kv_lora, nq=nq)

    rows = -(-(batch + 1) // 8) * 8
    cc = jnp.zeros((rows, d), F32).at[:batch].set(c).at[batch].set(c_ctx)
    mods = _ada(cc, ada_w, ada_b)

    cos_t, sin_t = _rope_tables(seq, rope)
    ones_t = jnp.concatenate([jnp.ones((n_ctx, rope), F32), jnp.zeros((n_ctx, LANE - rope), F32)], axis=-1)
    zeros_t = jnp.zeros((n_ctx, LANE), F32)
    tab_x = (cos_t, sin_t, True)
    tab_c = (ones_t, zeros_t, False)

    xs = x.reshape(batch * seq, d)
    cs = ctx.reshape(batch * n_ctx, d)
    for l in range(depth):
        last = l == depth - 1
        lw = _layer_weights(l, p, dims)
        mods_x = mods[l, :batch].reshape(batch, 1, -1)
        mods_c = mods[l, batch:batch + 1].reshape(1, 1, -1)
        hc, qc, kc, vc = _token_mixer_inputs(cs, mods_c, batch * n_ctx, n_ctx, l, p, lw, dims, tab_c)
        hx, qx, kx, vx = _token_mixer_inputs(xs, mods_x, seq, seq, l, p, lw, dims, tab_x)
        att_x = _attention(qx, kc, vc, kx, vx, batch, heads)
        xs_new = _finish_layer(xs, mods_x, seq, seq, l, p, lw, dims, hx, att_x)
        if not last:
            att_c = _attention(qc, kc, vc, None, None, batch, heads)
            cs = _finish_layer(cs, mods_c, batch * n_ctx, n_ctx, l, p, lw, dims, hc, att_c)
        xs = xs_new
    return xs.reshape(batch, seq, d)
```

```python
import functools

import jax
import jax.numpy as jnp
from jax import lax
from jax.experimental import pallas as pl
from jax.experimental.pallas import tpu as pltpu

GRID_W = 64
ROPE_THETA = 10000.0
POOL_WINDOWS = (2, 4, 8, 16)
EPS = 1e-6
LN_EPS = 1e-5

LOG2E = 1.4426950408889634
LANE = 128
SUBLANES = 8
HALO = 16
VMEM_LIMIT = 56 * 1024 * 1024

F32 = jnp.float32
BF16 = jnp.bfloat16


def _cparams(*sem):
    return pltpu.CompilerParams(dimension_semantics=sem, vmem_limit_bytes=VMEM_LIMIT)


def _sigmoid(x):
    return 0.5 * jnp.tanh(0.5 * x) + 0.5


def _pick(n, pref):
    if n <= pref:
        return n
    t = pref
    while n % t:
        t //= 2
    return t


def _ada_body(cc_ref, w_ref, b_ref, o_ref):
    cc = cc_ref[...]
    s = (cc * jax.nn.sigmoid(cc)).astype(BF16)
    o_ref[...] = jnp.dot(s, w_ref[...].astype(BF16), preferred_element_type=F32) + b_ref[...]


def _ada(cc, ada_w, ada_b):
    depth, d, n = ada_w.shape
    rows = cc.shape[0]
    tn = _pick(n, 1024)
    return pl.pallas_call(
        _ada_body,
        grid=(depth, n // tn),
        in_specs=[
            pl.BlockSpec((rows, d), lambda l, j: (0, 0)),
            pl.BlockSpec((None, d, tn), lambda l, j: (l, 0, j)),
            pl.BlockSpec((None, 1, tn), lambda l, j: (l, 0, j)),
        ],
        out_specs=pl.BlockSpec((None, rows, tn), lambda l, j: (l, 0, j)),
        out_shape=jax.ShapeDtypeStruct((depth, rows, n), F32),
        compiler_params=_cparams("arbitrary", "arbitrary"),
        name="ada",
    )(cc, ada_w, ada_b.reshape(depth, 1, n))


def _mm_res_body(a_ref, w_ref, r_ref, gate_ref, o_ref):
    acc = jnp.dot(a_ref[...], w_ref[...], preferred_element_type=F32)
    o_ref[...] = r_ref[...] + gate_ref[...] * acc


def _mm_res(a, w, res, mods, k_gate, rows_per_mod, *, tm, tn):
    m, k = a.shape
    n = w.shape[1]
    tm = _pick(rows_per_mod, tm)
    tn = _pick(n, tn)
    nj = n // tn
    return pl.pallas_call(
        _mm_res_body,
        grid=(m // tm, nj),
        in_specs=[
            pl.BlockSpec((tm, k), lambda i, j: (i, 0)),
            pl.BlockSpec((k, tn), lambda i, j: (0, j)),
            pl.BlockSpec((tm, tn), lambda i, j: (i, j)),
            pl.BlockSpec((None, 1, tn), lambda i, j: (i * tm // rows_per_mod, 0, k_gate * nj + j)),
        ],
        out_specs=pl.BlockSpec((tm, tn), lambda i, j: (i, j)),
        out_shape=jax.ShapeDtypeStruct((m, n), F32),
        compiler_params=_cparams("arbitrary", "arbitrary"),
        name="mm_res",
    )(a, w, res, mods)


def _halo_specs(tl, width, col_block, n_rows):
    per = tl // HALO
    last = n_rows // HALO - 1
    prev = pl.BlockSpec((HALO, width), lambda i, *_: (jnp.maximum(i * per - 1, 0), col_block))
    nxt = pl.BlockSpec((HALO, width), lambda i, *_: (jnp.minimum((i + 1) * per, last), col_block))
    return prev, nxt


def _seq_edges(tps):
    s = pl.program_id(0) % tps
    return s, s == 0, s == tps - 1


def _fill_lhs(lhs_ref, hp_ref, h_ref, hn_ref, first, last, tm):
    zero = jnp.zeros((HALO, lhs_ref.shape[1]), lhs_ref.dtype)
    lhs_ref[0:HALO, :] = jnp.where(first, zero, hp_ref[...])
    lhs_ref[HALO:HALO + tm, :] = h_ref[...]
    lhs_ref[HALO + tm:HALO + tm + HALO, :] = jnp.where(last, zero, hn_ref[...])


def _h_specs(tm, d, m):
    prev, nxt = _halo_specs(tm, d, 0, m)
    return [pl.BlockSpec((tm, d), lambda i, *_: (i, 0)), prev, nxt]


def _proj_conv_a_body(h_ref, hp_ref, hn_ref, wcg_ref, whh_ref, wb_ref, cw_ref, o_ref, lhs_ref, *, tm, tps):
    _, first, last = _seq_edges(tps)

    @pl.when(pl.program_id(1) == 0)
    def _():
        _fill_lhs(lhs_ref, hp_ref, h_ref, hn_ref, first, last, tm)

    lhs = lhs_ref[...]
    u = (jnp.dot(lhs, wcg_ref[...], preferred_element_type=F32)
         * jnp.dot(lhs, whh_ref[...], preferred_element_type=F32))
    b = jnp.dot(lhs_ref[HALO:HALO + tm, :], wb_ref[...], preferred_element_type=F32)
    taps = cw_ref.shape[0]
    pad = taps // 2
    conv = cw_ref[0:1, :] * u[HALO - pad:HALO - pad + tm, :]
    for t in range(1, taps):
        conv = conv + cw_ref[t:t + 1, :] * u[HALO - pad + t:HALO - pad + t + tm, :]
    o_ref[...] = (b * conv).astype(o_ref.dtype)


def _proj_conv_a(h, w_z, conv_w, width, seq_len):
    m, d = h.shape
    tm = _pick(seq_len, 1024)
    tc = _pick(width, 512)
    nb = width // tc

    def wcol(group):
        return pl.BlockSpec((d, tc), lambda i, j: (0, group * nb + j))

    return pl.pallas_call(
        functools.partial(_proj_conv_a_body, tm=tm, tps=seq_len // tm),
        grid=(m // tm, nb),
        in_specs=_h_specs(tm, d, m) + [wcol(2), wcol(3), wcol(4),
                                       pl.BlockSpec((conv_w.shape[0], tc), lambda i, j: (0, j))],
        out_specs=pl.BlockSpec((tm, tc), lambda i, j: (i, j)),
        out_shape=jax.ShapeDtypeStruct((m, width), BF16),
        scratch_shapes=[pltpu.VMEM((tm + 2 * HALO, d), BF16)],
        compiler_params=_cparams("arbitrary", "arbitrary"),
        name="proj_conv_a",
    )(h, h, h, w_z, w_z, w_z, conv_w)


def _window_sum(ext, win, tm):
    half = win // 2
    part, length, span = ext, ext.shape[0], 1
    while span < half:
        length -= SUBLANES
        part = part[0:length, :] + part[span:span + length, :]
        span *= 2
    return part[HALO - half:HALO - half + tm, :] + part[HALO:HALO + tm, :]


def _proj_pool_body(h_ref, hp_ref, hn_ref, w_ref, o_ref, lhs_ref, *, tm, tps, seq_len, group):
    s, first, last = _seq_edges(tps)
    _fill_lhs(lhs_ref, hp_ref, h_ref, hn_ref, first, last, tm)
    u = jnp.dot(lhs_ref[...], w_ref[...], preferred_element_type=F32)
    u = jnp.concatenate([u, jnp.zeros((SUBLANES, u.shape[1]), F32)], axis=0)
    pos = s * tm + lax.broadcasted_iota(jnp.int32, (tm, 1), 0)
    for g, win in enumerate(POOL_WINDOWS):
        cols = slice(g * group, (g + 1) * group)
        ext = u[:, cols]
        lo = jnp.maximum(pos - win // 2, 0)
        hi = jnp.minimum(pos - win // 2 + win, seq_len)
        cnt = (hi - lo).astype(F32)
        o_ref[:, cols] = (_window_sum(ext, win, tm) / cnt - ext[HALO:HALO + tm, :]).astype(o_ref.dtype)


def _proj_pool(h, w_z, width, seq_len):
    m, d = h.shape
    assert max(POOL_WINDOWS) // 2 <= SUBLANES and max(POOL_WINDOWS) <= HALO
    tm = _pick(seq_len, 512)
    return pl.pallas_call(
        functools.partial(_proj_pool_body, tm=tm, tps=seq_len // tm, seq_len=seq_len,
                          group=width // len(POOL_WINDOWS)),
        grid=(m // tm,),
        in_specs=_h_specs(tm, d, m) + [pl.BlockSpec((d, width), lambda i: (0, 5))],
        out_specs=pl.BlockSpec((tm, width), lambda i: (i, 0)),
        out_shape=jax.ShapeDtypeStruct((m, width), BF16),
        scratch_shapes=[pltpu.VMEM((tm + 2 * HALO, d), BF16)],
        compiler_params=_cparams("arbitrary"),
        name="proj_pool",
    )(h, h, h, w_z)


def _proj_conv_d_body(h_ref, hp_ref, hn_ref, wa_ref, wgt_ref, w_ref, cb_ref, lg_ref, lb_ref, o_ref,
                      lhs_ref, ext_ref, sh_ref, *, tl, tps):
    _, first, last = _seq_edges(tps)
    _fill_lhs(lhs_ref, hp_ref, h_ref, hn_ref, first, last, tl)
    lhs = lhs_ref[...]
    rows = sh_ref.shape[1]
    taps = w_ref.shape[0]
    pad = taps // 2
    slab = 2 * LANE
    ys = []
    for c0 in range(0, w_ref.shape[1], slab):
        cols = slice(c0, c0 + slab)
        ext_ref[:, cols] = (jnp.dot(lhs, wa_ref[:, cols], preferred_element_type=F32)
                            * _sigmoid(jnp.dot(lhs, wgt_ref[:, cols], preferred_element_type=F32)))
        for r in range(SUBLANES - 1):
            sh_ref[r, :, cols] = ext_ref[r + 1:r + 1 + rows, cols]
        acc = None
        for k in range(taps):
            off = HALO - pad + k
            if off % SUBLANES == 0:
                tap = ext_ref[off:off + tl, cols]
            else:
                base = off - off % SUBLANES
                tap = sh_ref[off % SUBLANES - 1, base:base + tl, cols]
            term = w_ref[k:k + 1, cols] * tap
            acc = term if acc is None else acc + term
        ys.append(acc)
    y = jnp.concatenate(ys, axis=1) + cb_ref[...]
    mu = jnp.mean(y, axis=-1, keepdims=True)
    yc = y - mu
    var = jnp.mean(yc * yc, axis=-1, keepdims=True)
    yn = yc * lax.rsqrt(var + LN_EPS) * lg_ref[...] + lb_ref[...]
    o_ref[...] = (yn * _sigmoid(yn)).astype(o_ref.dtype)


def _proj_conv_d(h, w_z, conv_w, conv_b, ln_g, ln_b, width, seq_len):
    m, d = h.shape
    tl = _pick(seq_len, 256)
    vec = pl.BlockSpec((1, width), lambda i: (0, 0))
    return pl.pallas_call(
        functools.partial(_proj_conv_d_body, tl=tl, tps=seq_len // tl),
        grid=(m // tl,),
        in_specs=_h_specs(tl, d, m) + [
            pl.BlockSpec((d, width), lambda i: (0, 0)),
            pl.BlockSpec((d, width), lambda i: (0, 1)),
            pl.BlockSpec(conv_w.shape, lambda i: (0, 0)),
            vec, vec, vec,
        ],
        out_specs=pl.BlockSpec((tl, width), lambda i: (i, 0)),
        out_shape=jax.ShapeDtypeStruct((m, width), BF16),
        scratch_shapes=[pltpu.VMEM((tl + 2 * HALO, d), BF16),
                        pltpu.VMEM((tl + 2 * HALO, width), F32),
                        pltpu.VMEM((SUBLANES - 1, tl + 2 * HALO - SUBLANES, width), F32)],
        compiler_params=_cparams("arbitrary"),
        name="proj_conv_d",
    )(h, h, h, w_z, w_z, conv_w, conv_b.reshape(1, width), ln_g.reshape(1, width), ln_b.reshape(1, width))


def _ffn_up_body(x_ref, xp_ref, xn_ref, g_ref, sh_ref, sc_ref, wg_ref, wv_ref, cw_ref, cb_ref, f_ref, h_ref,
                 *, tm, tps, seq_len):
    _, first, last = _seq_edges(tps)

    @pl.when(pl.program_id(1) == 0)
    def _():
        gain = g_ref[...] * (1.0 + sc_ref[...])

        def norm(x):
            ms = jnp.mean(x * x, axis=-1, keepdims=True)
            return (x * lax.rsqrt(ms + EPS) * gain + sh_ref[...]).astype(h_ref.dtype)

        zero = jnp.zeros((HALO, h_ref.shape[1]), h_ref.dtype)
        h_ref[0:HALO, :] = jnp.where(first, zero, norm(xp_ref[...]))
        h_ref[HALO:HALO + tm, :] = norm(x_ref[...])
        h_ref[HALO + tm:HALO + tm + HALO, :] = jnp.where(last, zero, norm(xn_ref[...]))

    gate = jnp.dot(h_ref[...], wg_ref[...], preferred_element_type=F32)
    val = jnp.dot(h_ref[HALO:HALO + tm, :], wv_ref[...], preferred_element_type=F32)
    taps = cw_ref.shape[0]
    pad = taps // 2
    pos = lax.broadcasted_iota(jnp.int32, (tm, 1), 0) % seq_len
    conv = None
    for t in range(taps):
        tap = gate[HALO - pad + t:HALO - pad + t + tm, :]
        if tm > seq_len and t != pad:
            src = pos + (t - pad)
            tap = jnp.where((src >= 0) & (src < seq_len), tap, 0.0)
        term = cw_ref[t:t + 1, :] * tap
        conv = term if conv is None else conv + term
    g = conv + cb_ref[...]
    f_ref[...] = (g * _sigmoid(g) * val).astype(f_ref.dtype)


def _ffn_up(x, gain, mods, k_shift, rows_per_mod, seq_len, w_up, conv_w, conv_b):
    m, d = x.shape
    d_ff = conv_w.shape[1]
    tc = _pick(d_ff, 512)
    nj = d_ff // tc
    tm = _pick(seq_len, 1024)
    if tm == seq_len:
        span = max(s for s in range(1, 1024 // tm + 1) if rows_per_mod % (s * tm) == 0)
        tm *= span
    per = tm // HALO
    last_blk = m // HALO - 1
    return pl.pallas_call(
        functools.partial(_ffn_up_body, tm=tm, tps=max(seq_len // tm, 1), seq_len=seq_len),
        grid=(m // tm, nj),
        in_specs=[
            pl.BlockSpec((tm, d), lambda i, j: (i, 0)),
            pl.BlockSpec((HALO, d), lambda i, j: (jnp.maximum(i * per - 1, 0), 0)),
            pl.BlockSpec((HALO, d), lambda i, j: (jnp.minimum((i + 1) * per, last_blk), 0)),
            pl.BlockSpec((1, d), lambda i, j: (0, 0)),
            pl.BlockSpec((None, 1, d), lambda i, j: (i * tm // rows_per_mod, 0, k_shift)),
            pl.BlockSpec((None, 1, d), lambda i, j: (i * tm // rows_per_mod, 0, k_shift + 1)),
            pl.BlockSpec((d, tc), lambda i, j: (0, j)),
            pl.BlockSpec((d, tc), lambda i, j: (0, nj + j)),
            pl.BlockSpec((conv_w.shape[0], tc), lambda i, j: (0, j)),
            pl.BlockSpec((1, tc), lambda i, j: (0, j)),
        ],
        out_specs=pl.BlockSpec((tm, tc), lambda i, j: (i, j)),
        out_shape=jax.ShapeDtypeStruct((m, d_ff), BF16),
        scratch_shapes=[pltpu.VMEM((tm + 2 * HALO, d), BF16)],
        compiler_params=_cparams("arbitrary", "arbitrary"),
        name="ffn_up",
    )(x, x, x, gain.reshape(1, d), mods, mods, w_up, w_up, conv_w, conv_b.reshape(1, d_ff))


def _rope_vreg(bn, cos_ref, sin_ref):
    return bn * cos_ref[...] + pltpu.roll(bn, LANE // 2, 1) * sin_ref[...]


def _norm_qkv_body(x_ref, g_ref, sh_ref, sc_ref, wz_ref, qg_ref, wq_ref, kvg_ref, wkv_ref, qhg_ref, khg_ref,
                   cos_ref, sin_ref, h_ref, q_ref, k_ref, v_ref, *, q_lora, kv_lora, heads, qk_dim, scale):
    x = x_ref[...]
    y = x * lax.rsqrt(jnp.mean(x * x, axis=-1, keepdims=True) + EPS) * g_ref[...]
    h = (y * (1.0 + sc_ref[...]) + sh_ref[...]).astype(h_ref.dtype)
    h_ref[...] = h
    z = jnp.dot(h, wz_ref[...], preferred_element_type=F32)
    lane = lax.broadcasted_iota(jnp.int32, (1, LANE), 1)
    rope_lanes = lane < (LANE // 2)
    slot = 2 * LANE

    def rms(v, g):
        return v * lax.rsqrt(jnp.mean(v * v, axis=-1, keepdims=True) + EPS) * g

    qn = rms(z[:, :q_lora], qg_ref[...]).astype(BF16)
    yq = jnp.dot(qn, wq_ref[...], preferred_element_type=F32)
    g_nope, g_rope = qhg_ref[:, :LANE], qhg_ref[:, LANE:]
    for h in range(heads):
        a = yq[:, h * slot:h * slot + LANE]
        b = yq[:, h * slot + LANE:(h + 1) * slot]
        ss = jnp.sum(a * a + jnp.where(rope_lanes, b * b, 0.0), axis=-1, keepdims=True)
        r = lax.rsqrt(ss / qk_dim + EPS) * scale
        q_ref[:, h * slot:h * slot + LANE] = (a * r * g_nope).astype(q_ref.dtype)
        q_ref[:, h * slot + LANE:(h + 1) * slot] = _rope_vreg(b * r * g_rope, cos_ref, sin_ref).astype(q_ref.dtype)

    kvn = rms(z[:, q_lora:q_lora + kv_lora], kvg_ref[...]).astype(BF16)
    ykv = jnp.dot(kvn, wkv_ref[...], preferred_element_type=F32)
    kr = z[:, q_lora + kv_lora:q_lora + kv_lora + LANE]
    kr_ss = jnp.sum(jnp.where(rope_lanes, kr * kr, 0.0), axis=-1, keepdims=True)
    g_nope, g_rope = khg_ref[:, :LANE], khg_ref[:, LANE:]
    for h in range(heads):
        a = ykv[:, h * LANE:(h + 1) * LANE]
        ss = jnp.sum(a * a, axis=-1, keepdims=True) + kr_ss
        r = lax.rsqrt(ss / qk_dim + EPS)
        k_ref[:, h * slot:h * slot + LANE] = (a * r * g_nope).astype(k_ref.dtype)
        k_ref[:, h * slot + LANE:(h + 1) * slot] = _rope_vreg(kr * r * g_rope, cos_ref, sin_ref).astype(k_ref.dtype)
    ones = jnp.ones((z.shape[0], LANE), v_ref.dtype)
    for h in range(heads):
        v_ref[:, h * slot:h * slot + LANE] = ykv[:, (heads + h) * LANE:(heads + h + 1) * LANE].astype(v_ref.dtype)
        v_ref[:, h * slot + LANE:(h + 1) * slot] = ones


def _norm_qkv(x, gain, mods, k_shift, rows_per_mod, lw, cos_t, sin_t, seq_len, dims, use_rope):
    m, d = x.shape
    heads, q_lora, kv_lora, qk_dim = dims["heads"], dims["q_lora"], dims["kv_lora"], dims["qk_dim"]
    w_zq = lw["w_z"][:, -dims["nq"]:]
    tm = _pick(seq_len, 512)
    tps = seq_len // tm
    if use_rope:
        tab = pl.BlockSpec((tm, LANE), lambda i: (i % tps, 0))
    else:
        tab = pl.BlockSpec((tm, LANE), lambda i: (0, 0))

    def full(a):
        return pl.BlockSpec(a.shape, lambda i: (0, 0))

    qw = heads * 2 * LANE
    row = pl.BlockSpec((tm, qw), lambda i: (i, 0))
    return pl.pallas_call(
        functools.partial(_norm_qkv_body, q_lora=q_lora, kv_lora=kv_lora, heads=heads, qk_dim=qk_dim,
                          scale=qk_dim ** -0.5 * LOG2E),
        grid=(m // tm,),
        in_specs=[
            pl.BlockSpec((tm, d), lambda i: (i, 0)),
            pl.BlockSpec((1, d), lambda i: (0, 0)),
            pl.BlockSpec((None, 1, d), lambda i: (i * tm // rows_per_mod, 0, k_shift)),
            pl.BlockSpec((None, 1, d), lambda i: (i * tm // rows_per_mod, 0, k_shift + 1)),
            full(w_zq),
            full(lw["q_norm_g"]), full(lw["wq"]), full(lw["kv_norm_g"]), full(lw["wkv"]),
            full(lw["q_head_g"]), full(lw["k_head_g"]), tab, tab,
        ],
        out_specs=[pl.BlockSpec((tm, d), lambda i: (i, 0)), row, row, row],
        out_shape=[jax.ShapeDtypeStruct((m, d), BF16)] + [jax.ShapeDtypeStruct((m, qw), BF16)] * 3,
        compiler_params=_cparams("arbitrary"),
        name="norm_qkv",
    )(x, gain.reshape(1, d), mods, mods, w_zq, lw["q_norm_g"], lw["wq"], lw["kv_norm_g"], lw["wkv"],
      lw["q_head_g"], lw["k_head_g"], cos_t, sin_t)


def _attn_update(q, k, v1, carry):
    m_i, acc = carry
    s = lax.dot_general(q, k, (((1,), (1,)), ((), ())), preferred_element_type=F32)
    m_new = jnp.maximum(m_i, jnp.max(s, axis=-1, keepdims=True))
    alpha = jnp.exp2(m_i - m_new)
    p = jnp.exp2(s - m_new)
    acc = alpha * acc + jnp.dot(p.astype(v1.dtype), v1, preferred_element_type=F32)
    return m_new, acc


def _attn_body(*refs, tk, n_lat, hps):
    if n_lat:
        q_ref, kc_ref, vc_ref, kx_ref, vx_ref, o_ref = refs
    else:
        q_ref, kc_ref, vc_ref, o_ref = refs
    tq = q_ref.shape[0]
    slot, vd = 2 * LANE, LANE
    heads = [slice(h * slot, (h + 1) * slot) for h in range(hps)]
    qs = [q_ref[:, hs] for hs in heads]
    carries = [(jnp.full((tq, 1), -jnp.inf, F32), jnp.zeros((tq, slot), F32)) for _ in heads]
    carries = [_attn_update(q, kc_ref[:, hs], vc_ref[:, hs], c) for q, hs, c in zip(qs, heads, carries)]
    for c in range(n_lat):
        rows = slice(c * tk, (c + 1) * tk)
        carries = [_attn_update(q, kx_ref[rows, hs], vx_ref[rows, hs], cr)
                   for q, hs, cr in zip(qs, heads, carries)]
    for h, (_, acc) in enumerate(carries):
        o_ref[:, h * vd:(h + 1) * vd] = (acc[:, :vd] / acc[:, vd:]).astype(o_ref.dtype)


def _attention(q, kc, vc, kx, vx, batch, heads):
    lq = q.shape[0] // batch
    lc = kc.shape[0] // batch
    vd = LANE
    hps = 2 if heads % 2 == 0 else 1
    slot = hps * 2 * LANE
    tq = _pick(lq, 1024)
    q3 = q.reshape(batch, lq, heads * 2 * LANE)
    args = [q3, kc.reshape(batch, lc, heads * 2 * LANE), vc.reshape(batch, lc, heads * 2 * LANE)]
    in_specs = [
        pl.BlockSpec((None, tq, slot), lambda b, h, i: (b, i, h)),
        pl.BlockSpec((None, lc, slot), lambda b, h, i: (b, 0, h)),
        pl.BlockSpec((None, lc, slot), lambda b, h, i: (b, 0, h)),
    ]
    n_lat, tk = 0, 0
    if kx is not None:
        lx = kx.shape[0] // batch
        tk = _pick(lx, 1024)
        n_lat = lx // tk
        args += [kx.reshape(batch, lx, heads * 2 * LANE), vx.reshape(batch, lx, heads * 2 * LANE)]
        in_specs += [
            pl.BlockSpec((None, lx, slot), lambda b, h, i: (b, 0, h)),
            pl.BlockSpec((None, lx, slot), lambda b, h, i: (b, 0, h)),
        ]
    out = pl.pallas_call(
        functools.partial(_attn_body, tk=tk, n_lat=n_lat, hps=hps),
        grid=(batch, heads // hps, lq // tq),
        in_specs=in_specs,
        out_specs=pl.BlockSpec((None, tq, hps * vd), lambda b, h, i: (b, i, h)),
        out_shape=jax.ShapeDtypeStruct((batch, lq, heads * vd), BF16),
        compiler_params=_cparams("arbitrary", "arbitrary", "arbitrary"),
        name="attention_x" if n_lat else "attention_c",
    )(*args)
    return out.reshape(batch * lq, heads * vd)


def _merge_body(h_ref, a_ref, t_ref, p_ref, d_ref, wga_ref, wgt_ref, wgp_ref, wgd_ref, wa_ref, wt_ref, wp_ref,
                ps_ref, wd_ref, o_ref):
    h = h_ref[...]

    def gate(wg_ref):
        return _sigmoid(jnp.dot(h, wg_ref[...], preferred_element_type=F32))

    def proj(x_ref, w):
        return jnp.dot(x_ref[...], w, preferred_element_type=F32)

    acc = gate(wga_ref) * proj(a_ref, wa_ref[...])
    acc = acc + gate(wgt_ref) * proj(t_ref, wt_ref[...])
    acc = acc + gate(wgp_ref) * (proj(p_ref, wp_ref[...]) * ps_ref[...])
    acc = acc + gate(wgd_ref) * proj(d_ref, wd_ref[...])
    o_ref[...] = acc.astype(o_ref.dtype)


def _merge(h, act_a, att, act_p, act_d, lw, rows_per_seq):
    m, d = h.shape
    n_branch = lw["wg"].shape[1] // d
    assert n_branch == 4 and len(POOL_WINDOWS) == n_branch
    tn = d // n_branch
    group = act_p.shape[1] // n_branch
    tm = _pick(rows_per_seq, 512)

    def act(a):
        return pl.BlockSpec((tm, a.shape[1]), lambda j, i: (i, 0))

    def wcol(w):
        return pl.BlockSpec((w.shape[0], tn), lambda j, i: (0, j))

    return pl.pallas_call(
        _merge_body,
        grid=(n_branch, m // tm),
        in_specs=[
            act(h), act(act_a), act(att),
            pl.BlockSpec((tm, group), lambda j, i: (i, j)),
            act(act_d),
            *[pl.BlockSpec((d, tn), lambda j, i, b=b: (0, b * n_branch + j)) for b in range(n_branch)],
            wcol(lw["w_a_out"]), wcol(lw["w_mla_out"]),
            pl.BlockSpec((None, group, tn), lambda j, i: (j, 0, 0)),
            pl.BlockSpec((1, tn), lambda j, i: (0, j)),
            wcol(lw["w_d_out"]),
        ],
        out_specs=pl.BlockSpec((tm, tn), lambda j, i: (i, j)),
        out_shape=jax.ShapeDtypeStruct((m, d), BF16),
        compiler_params=_cparams("arbitrary", "arbitrary"),
        name="merge",
    )(h, act_a, att, act_p, act_d, lw["wg"], lw["wg"], lw["wg"], lw["wg"], lw["w_a_out"], lw["w_mla_out"],
      lw["w_pool"], lw["pool_scale"], lw["w_d_out"])


def _rope_perm(rope_dim):
    nf = rope_dim // 4
    j = jnp.arange(rope_dim)
    return jnp.where((j % (2 * nf)) < nf, j + nf, j - nf)


def _rope_tables(seq_len, rope_dim):
    rows = seq_len // GRID_W
    row = jnp.broadcast_to(jnp.arange(rows)[:, None], (rows, GRID_W)).reshape(seq_len)
    col = jnp.broadcast_to(jnp.arange(GRID_W)[None, :], (rows, GRID_W)).reshape(seq_len)
    nf = rope_dim // 4
    inv = ROPE_THETA ** (-jnp.arange(nf, dtype=F32) / nf)
    ang = jnp.stack([row, col], axis=-1).astype(F32)[:, :, None] * inv
    cos, sin = jnp.cos(ang), jnp.sin(ang)
    cos_t = jnp.concatenate([cos, cos], axis=-1).reshape(seq_len, rope_dim)
    sin_t = jnp.concatenate([-sin, sin], axis=-1).reshape(seq_len, rope_dim)
    pad = jnp.zeros((seq_len, LANE - rope_dim), F32)
    return jnp.concatenate([cos_t, pad], axis=-1), jnp.concatenate([sin_t, pad], axis=-1)


def _layer_weights(l, p, dims):
    d, wa, wc, wd = dims["d"], dims["wa"], dims["wc"], dims["wd"]
    heads, nope, rope, vdim = dims["heads"], dims["nope"], dims["rope"], dims["vdim"]
    q_lora, kv_lora, nq = dims["q_lora"], dims["kv_lora"], dims["nq"]
    perm = _rope_perm(rope)
    w_in = p["w_in"][l].astype(BF16)
    off_q = 3 * wa
    off_kv = off_q + q_lora
    off_kr = off_kv + kv_lora
    off_p = off_kr + rope
    off_d = off_p + wc
    off_g = off_d + 2 * wd
    kr = w_in[:, off_kr:off_p]
    qkv_used = q_lora + kv_lora + 2 * rope
    w_z = jnp.concatenate([
        w_in[:, off_d:off_g],
        w_in[:, wa:3 * wa],
        w_in[:, :wa],
        w_in[:, off_p:off_d],
        w_in[:, off_q:off_kr],
        kr, kr[:, perm],
        jnp.zeros((d, nq - qkv_used), w_in.dtype),
    ], axis=1)
    wg = w_in[:, off_g:]

    qk = nope + rope
    wq = p["w_q_up"][l].astype(BF16).reshape(q_lora, heads, qk)
    wq = jnp.concatenate([wq, wq[:, :, nope:][:, :, perm]], axis=-1).reshape(q_lora, heads * 2 * LANE)
    wkv = p["w_kv_up"][l].astype(BF16).reshape(kv_lora, heads, nope + vdim)
    wkv = jnp.concatenate([wkv[:, :, :nope].reshape(kv_lora, heads * nope),
                           wkv[:, :, nope:].reshape(kv_lora, heads * vdim)], axis=1)

    def head_gain(g):
        return jnp.concatenate([g, g[nope:][perm]]).reshape(1, 2 * LANE)

    return {
        "w_z": w_z, "wg": wg,
        "wq": wq, "wkv": wkv,
        "q_norm_g": p["q_norm_g"][l].reshape(1, q_lora), "kv_norm_g": p["kv_norm_g"][l].reshape(1, kv_lora),
        "q_head_g": head_gain(p["q_head_g"][l]), "k_head_g": head_gain(p["k_head_g"][l]),
        "w_a_out": p["w_a_out"][l].astype(BF16), "w_mla_out": p["w_mla_out"][l].astype(BF16),
        "w_pool": p["w_pool"][l].astype(BF16), "pool_scale": p["pool_scale"][l].reshape(1, d),
        "w_d_out": p["w_d_out"][l].astype(BF16), "w_out": p["w_out"][l].astype(BF16),
        "w_up": p["w_up"][l].astype(BF16), "w_down": p["w_down"][l].astype(BF16),
    }


def _token_mixer_inputs(xs, mods, rows_per_mod, seq_len, l, p, lw, dims, tables):
    return _norm_qkv(xs, p["norm1_g"][l], mods, 0, rows_per_mod, lw, tables[0], tables[1], seq_len, dims, tables[2])


def _finish_layer(xs, mods, rows_per_mod, seq_len, l, p, lw, dims, h, att):
    wa, wc, wd = dims["wa"], dims["wc"], dims["wd"]
    act_a = _proj_conv_a(h, lw["w_z"], p["conv_a_w"][l], wa, seq_len)
    act_p = _proj_pool(h, lw["w_z"], wc, seq_len)
    act_d = _proj_conv_d(h, lw["w_z"], p["conv_d_w"][l], p["conv_d_b"][l], p["cd_ln_g"][l], p["cd_ln_b"][l],
                         wd, seq_len)
    merged = _merge(h, act_a, att, act_p, act_d, lw, rows_per_mod)
    x1 = _mm_res(merged, lw["w_out"], xs, mods, 2, rows_per_mod, tm=512, tn=2048)
    f = _ffn_up(x1, p["norm2_g"][l], mods, 3, rows_per_mod, seq_len, lw["w_up"], p["conv_ff_w"][l],
                p["conv_ff_b"][l])
    return _mm_res(f, lw["w_down"], x1, mods, 5, rows_per_mod, tm=1024, tn=512)


def kernel(x, c, ctx, c_ctx, ada_w, ada_b, norm1_g, w_in, conv_a_w, w_a_out, q_norm_g, w_q_up, kv_norm_g, w_kv_up, q_head_g, k_head_g, w_mla_out, w_pool, pool_scale, conv_d_w, conv_d_b, cd_ln_g, cd_ln_b, w_d_out, w_out, norm2_g, w_up, conv_ff_w, conv_ff_b, w_down):
    p = dict(norm1_g=norm1_g, w_in=w_in, conv_a_w=conv_a_w, w_a_out=w_a_out, q_norm_g=q_norm_g, w_q_up=w_q_up,
             kv_norm_g=kv_norm_g, w_kv_up=w_kv_up, q_head_g=q_head_g, k_head_g=k_head_g, w_mla_out=w_mla_out,
             w_pool=w_pool, pool_scale=pool_scale, conv_d_w=conv_d_w, conv_d_b=conv_d_b, cd_ln_g=cd_ln_g,
             cd_ln_b=cd_ln_b, w_d_out=w_d_out, w_out=w_out, norm2_g=norm2_g, w_up=w_up, conv_ff_w=conv_ff_w,
             conv_ff_b=conv_ff_b, w_down=w_down)
    batch, seq, d = x.shape
    n_ctx = ctx.shape[1]
    depth = ada_w.shape[0]
    qk_dim = q_head_g.shape[1]
    heads = w_q_up.shape[2] // qk_dim
    vdim = w_mla_out.shape[1] // heads
    nope = w_kv_up.shape[2] // heads - vdim
    rope = qk_dim - nope
    wa, wc, wd = conv_a_w.shape[2], w_pool.shape[1] * w_pool.shape[2], conv_d_w.shape[2]
    q_lora, kv_lora = w_q_up.shape[1], w_kv_up.shape[1]
    assert nope == LANE and vdim == LANE and 2 * rope == LANE, "head layout assumes 128 | 64 | 128 dims"
    assert wa == wc == wd and wa % (len(POOL_WINDOWS) * LANE) == 0
    nq = q_lora + kv_lora + 2 * rope
    while (6 * wa) % nq or nq % LANE:
        nq += LANE
    dims = dict(d=d, wa=wa, wc=wc, wd=wd, heads=heads, nope=nope, rope=rope, vdim=vdim, qk_dim=qk_dim,
                q_lora=q_lora, kv_lora=kv_lora, nq=nq)

    rows = -(-(batch + 1) // 8) * 8
    cc = jnp.zeros((rows, d), F32).at[:batch].set(c).at[batch].set(c_ctx)
    mods = _ada(cc, ada_w, ada_b)

    cos_t, sin_t = _rope_tables(seq, rope)
    ones_t = jnp.concatenate([jnp.ones((n_ctx, rope), F32), jnp.zeros((n_ctx, LANE - rope), F32)], axis=-1)
    zeros_t = jnp.zeros((n_ctx, LANE), F32)
    tab_x = (cos_t, sin_t, True)
    tab_c = (ones_t, zeros_t, False)

    xs = x.reshape(batch * seq, d)
    cs = ctx.reshape(batch * n_ctx, d)
    for l in range(depth):
        last = l == depth - 1
        lw = _layer_weights(l, p, dims)
        mods_x = mods[l, :batch].reshape(batch, 1, -1)
        mods_c = mods[l, batch:batch + 1].reshape(1, 1, -1)
        hc, qc, kc, vc = _token_mixer_inputs(cs, mods_c, batch * n_ctx, n_ctx, l, p, lw, dims, tab_c)
        hx, qx, kx, vx = _token_mixer_inputs(xs, mods_x, seq, seq, l, p, lw, dims, tab_x)
        att_x = _attention(qx, kc, vc, kx, vx, batch, heads)
        xs_new = _finish_layer(xs, mods_x, seq, seq, l, p, lw, dims, hx, att_x)
        if not last:
            att_c = _attention(qc, kc, vc, None, None, batch, heads)
            cs = _finish_layer(cs, mods_c, batch * n_ctx, n_ctx, l, p, lw, dims, hc, att_c)
        xs = xs_new
    return xs.reshape(batch, seq, d)
```
